```python
import math
import jax, jax.numpy as jnp
from jax import lax
import numpy as np

D_MODEL = 1024
BATCH = 4
SEQ = 4096
DEPTH = 2
DEC_BATCH = 16
DEC_SEQ = 2048
PAST_LEN = 128

MIX_WIDTH = D_MODEL
HG_WIDTH = MIX_WIDTH // 2
HG_DK = 128
HG_HEADS = HG_WIDTH // HG_DK
HG_DV = HG_WIDTH // HG_HEADS
HG_CHUNK = 64
DA_WIDTH = MIX_WIDTH - HG_WIDTH
DA_HEADS = 4
DA_HEAD_DIM = DA_WIDTH // DA_HEADS // 2
DA_QBLOCK = 128
ROPE_THETA = 10000.0
FFN_HIDDEN = -(-8 * D_MODEL // (3 * 256)) * 256
NORM_EPS = 1e-6
SUBLN_EPS = 1e-5
LOG_FLOOR = 1e-30
IN_COLS = 5 * HG_WIDTH + 3 * DA_WIDTH
IN_SPLITS = tuple(HG_WIDTH * i for i in range(1, 6)) + tuple(5 * HG_WIDTH + DA_WIDTH * i for i in range(1, 3))

kernel_name = "hybrid_hgrn2_diffattn_encoder"


def _rms_norm(x, w, eps=NORM_EPS):
    xf = x.astype(jnp.float32)
    y = xf * lax.rsqrt(jnp.mean(xf * xf, axis=-1, keepdims=True) + eps)
    return (y * w.astype(jnp.float32)).astype(x.dtype)


def _rotary(x):
    T, dh = x.shape[1], x.shape[-1]
    inv = 1.0 / (ROPE_THETA ** (jnp.arange(0, dh, 2, dtype=jnp.float32) / dh))
    ang = jnp.arange(T, dtype=jnp.float32)[:, None] * inv[None, :]
    cos = jnp.cos(ang)[None, :, None, :]
    sin = jnp.sin(ang)[None, :, None, :]
    xf = x.astype(jnp.float32)
    x1, x2 = xf[..., : dh // 2], xf[..., dh // 2:]
    out = jnp.concatenate([x1 * cos - x2 * sin, x2 * cos + x1 * sin], axis=-1)
    return out.astype(x.dtype)


def _hgrn2_direction(q, k, v, log_f):
    B, T, H, DK = q.shape
    DV = v.shape[-1]
    n = T // HG_CHUNK

    def chunks(a):
        return a.reshape(B, n, HG_CHUNK, H, a.shape[-1]).transpose(1, 0, 3, 2, 4)

    causal = jnp.tril(jnp.ones((HG_CHUNK, HG_CHUNK), dtype=bool))[:, :, None]

    def step(state, inp):
        qc, kc, vc, gc = inp
        b = jnp.cumsum(gc, axis=2)
        diff = b[:, :, :, None, :] - b[:, :, None, :, :]
        decay = jnp.where(causal, jnp.exp(jnp.minimum(diff, 0.0)), 0.0)
        scores = jnp.einsum('bhtk,bhsk,bhtsk->bhts', qc, kc, decay)
        out = (jnp.einsum('bhts,bhsv->bhtv', scores, vc)
               + jnp.einsum('bhtk,bhkv->bhtv', qc * jnp.exp(b), state))
        b_end = b[:, :, -1:, :]
        state = (state * jnp.exp(b_end)[:, :, 0, :, None]
                 + jnp.einsum('bhsk,bhsv->bhkv', kc * jnp.exp(b_end - b), vc))
        return state, out

    s0 = jnp.zeros((B, H, DK, DV), jnp.float32)
    _, out = lax.scan(step, s0, (chunks(q), chunks(k), chunks(v), chunks(log_f)))
    return out.transpose(1, 0, 3, 2, 4).reshape(B, T, H, DV)


def _hgrn2_mixer(hq, hf_fwd, hf_bwd, hi, hg, lb, gnorm_w):
    B, T, _ = hq.shape
    shape_k = (B, T, HG_HEADS, HG_DK)
    q = jax.nn.silu(hq.astype(jnp.float32)).reshape(shape_k)
    v = hi.astype(jnp.float32).reshape(B, T, HG_HEADS, HG_DV)

    def gates(z, lb_dir):
        z = z.astype(jnp.float32).reshape(shape_k)
        lb_dir = lb_dir.reshape(HG_HEADS, HG_DK)
        log_f = jnp.logaddexp(jnp.log(jnp.maximum(lb_dir, LOG_FLOOR)),
                              jnp.log1p(-lb_dir) + jax.nn.log_sigmoid(z))
        k = (1.0 - lb_dir) * jax.nn.sigmoid(-z)
        return k, log_f

    k_f, lf_f = gates(hf_fwd, lb[0])
    k_b, lf_b = gates(hf_bwd, lb[1])
    flip = lambda a: jnp.flip(a, axis=1)
    o_f = _hgrn2_direction(q, k_f, v, lf_f)
    o_b = flip(_hgrn2_direction(flip(q), flip(k_b), flip(v), flip(lf_b)))
    gate = jax.nn.silu(hg.astype(jnp.float32)).reshape(B, T, HG_HEADS, HG_DV)
    o = _rms_norm(o_f + o_b, gnorm_w) * gate
    return o.reshape(B, T, HG_WIDTH).astype(hq.dtype)


def _diff_attention(hq, hk, hv, lq1, lk1, lq2, lk2, subln_w, lambda_init):
    B, T, _ = hq.shape
    q = _rotary(hq.reshape(B, T, 2 * DA_HEADS, DA_HEAD_DIM))
    k = _rotary(hk.reshape(B, T, 2 * DA_HEADS, DA_HEAD_DIM))
    v = hv.reshape(B, T, DA_HEADS, 2 * DA_HEAD_DIM)
    f32 = jnp.float32
    lam = (jnp.exp(jnp.sum(lq1.astype(f32) * lk1.astype(f32)))
           - jnp.exp(jnp.sum(lq2.astype(f32) * lk2.astype(f32))) + lambda_init)
    scale = DA_HEAD_DIM ** -0.5
    nq = T // DA_QBLOCK
    q_blocks = q.reshape(B, nq, DA_QBLOCK, 2 * DA_HEADS, DA_HEAD_DIM).transpose(1, 0, 2, 3, 4)

    def attend(qb):
        s = jnp.einsum('bqhd,bkhd->bhqk', qb, k).astype(f32) * scale
        p = jax.nn.softmax(s, axis=-1).reshape(B, DA_HEADS, 2, DA_QBLOCK, T)
        a = p[:, :, 0] - lam * p[:, :, 1]
        return jnp.einsum('bhqk,bkhv->bqhv', a.astype(v.dtype), v)

    o = lax.map(attend, q_blocks)
    o = o.transpose(1, 0, 2, 3, 4).reshape(B, T, DA_HEADS, 2 * DA_HEAD_DIM)
    o = _rms_norm(o, subln_w, SUBLN_EPS) * (1.0 - lambda_init)
    return o.reshape(B, T, DA_WIDTH)


def _layer(x, c, layer_idx, lb, w_ada, b_ada, norm_pre_mix, norm_post_mix, norm_pre_ffn,
           norm_post_ffn, w_in, hg_gnorm, lq1, lk1, lq2, lk2, da_subln, w_out,
           w_ffn_gate, w_ffn_up, w_ffn_down):
    mod = jax.nn.silu(c) @ w_ada + b_ada
    shift_m, scale_m, gate_m, shift_f, scale_f, gate_f = [m[:, None, :] for m in jnp.split(mod, 6, axis=-1)]
    h = _rms_norm(x, norm_pre_mix) * (1.0 + scale_m) + shift_m
    hq, hff, hfb, hi, hg, dq, dk, dv = jnp.split(h @ w_in, IN_SPLITS, axis=-1)
    o_hg = _hgrn2_mixer(hq, hff, hfb, hi, hg, lb, hg_gnorm)
    lambda_init = 0.8 - 0.6 * math.exp(-0.3 * layer_idx)
    o_da = _diff_attention(dq, dk, dv, lq1, lk1, lq2, lk2, da_subln, lambda_init)
    mix = jnp.concatenate([o_hg, o_da], axis=-1) @ w_out
    x = x + gate_m * _rms_norm(mix, norm_post_mix)
    h = _rms_norm(x, norm_pre_ffn) * (1.0 + scale_f) + shift_f
    ffn = (jax.nn.silu(h @ w_ffn_gate) * (h @ w_ffn_up)) @ w_ffn_down
    x = x + gate_f * _rms_norm(ffn, norm_post_ffn)
    return x


def setup_inputs(seed: int = 0) -> dict:
    key = jax.random.key(seed)
    ks = jax.random.split(key, 24)
    f32 = jnp.float32
    nrm = lambda k, shape, s: s * jax.random.normal(k, shape, f32)
    gain = lambda k, shape: 1.0 + 0.05 * jax.random.normal(k, shape, f32)
    return {
        "x_prompt": nrm(ks[0], (BATCH, SEQ, D_MODEL), 1.0),
        "x_sample": nrm(ks[1], (DEC_BATCH, DEC_SEQ, D_MODEL), 1.0),
        "c_prompt": nrm(ks[2], (BATCH, D_MODEL), 1.0),
        "c_sample": nrm(ks[3], (DEC_BATCH, D_MODEL), 1.0),
        "w_ada": nrm(ks[4], (DEPTH, D_MODEL, 6 * D_MODEL), 0.5 * D_MODEL ** -0.5),
        "b_ada": nrm(ks[5], (DEPTH, 6 * D_MODEL), 0.02),
        "norm_pre_mix": gain(ks[6], (DEPTH, D_MODEL)),
        "norm_post_mix": gain(ks[7], (DEPTH, D_MODEL)),
        "norm_pre_ffn": gain(ks[8], (DEPTH, D_MODEL)),
        "norm_post_ffn": gain(ks[9], (DEPTH, D_MODEL)),
        "w_in": nrm(ks[10], (DEPTH, D_MODEL, IN_COLS), D_MODEL ** -0.5),
        "hg_lower_bounds": 1.0 + 0.1 * jax.random.normal(ks[11], (2, DEPTH, HG_WIDTH), f32),
        "hg_gnorm": gain(ks[12], (DEPTH, HG_DV)),
        "da_lambda_q1": nrm(ks[13], (DEPTH, DA_HEAD_DIM), 0.1),
        "da_lambda_k1": nrm(ks[14], (DEPTH, DA_HEAD_DIM), 0.1),
        "da_lambda_q2": nrm(ks[15], (DEPTH, DA_HEAD_DIM), 0.1),
        "da_lambda_k2": nrm(ks[16], (DEPTH, DA_HEAD_DIM), 0.1),
        "da_subln": gain(ks[17], (DEPTH, 2 * DA_HEAD_DIM)),
        "w_out": nrm(ks[18], (DEPTH, MIX_WIDTH, D_MODEL), MIX_WIDTH ** -0.5),
        "w_ffn_gate": nrm(ks[19], (DEPTH, D_MODEL, FFN_HIDDEN), D_MODEL ** -0.5),
        "w_ffn_up": nrm(ks[20], (DEPTH, D_MODEL, FFN_HIDDEN), D_MODEL ** -0.5),
        "w_ffn_down": nrm(ks[21], (DEPTH, FFN_HIDDEN, D_MODEL), FFN_HIDDEN ** -0.5),
    }


def reference(x_prompt, x_sample, c_prompt, c_sample, w_ada, b_ada, norm_pre_mix, norm_post_mix,
              norm_pre_ffn, norm_post_ffn, w_in, hg_lower_bounds, hg_gnorm, da_lambda_q1,
              da_lambda_k1, da_lambda_q2, da_lambda_k2, da_subln, w_out, w_ffn_gate, w_ffn_up,
              w_ffn_down):
    lb_soft = jax.nn.softmax(hg_lower_bounds.astype(jnp.float32), axis=1)
    lb_all = jnp.cumsum(lb_soft, axis=1) - lb_soft[:, :1]

    def trunk(x, c):
        for l in range(DEPTH):
            x = _layer(x, c, l, lb_all[:, l], w_ada[l], b_ada[l], norm_pre_mix[l], norm_post_mix[l],
                       norm_pre_ffn[l], norm_post_ffn[l], w_in[l], hg_gnorm[l], da_lambda_q1[l],
                       da_lambda_k1[l], da_lambda_q2[l], da_lambda_k2[l], da_subln[l], w_out[l],
                       w_ffn_gate[l], w_ffn_up[l], w_ffn_down[l])
        return x

    y_prompt = trunk(x_prompt, c_prompt)
    y_sample = trunk(x_sample, c_sample)
    return (y_prompt, y_sample)
```

```python
import functools
import math

import numpy as np
import jax
import jax.numpy as jnp
from jax import lax
from jax.experimental import pallas as pl
from jax.experimental.pallas import tpu as pltpu

F32 = jnp.float32
BF16 = jnp.bfloat16

HG_WIDTH = 512
HG_HEADS = 4
HG_DK = 128
DA_WIDTH = 512
DA_HEADS = 4
DA_HEAD_DIM = 64
ROPE_THETA = 10000.0
NORM_EPS = 1e-6
SUBLN_EPS = 1e-5
LOG_FLOOR = 1e-30
N_MOD = 6

LANES = 128
SUBLANES = 8
VMEM_LIMIT_BYTES = 56 * 1024 * 1024

COL_HQ, COL_HFF, COL_HFB, COL_HI, COL_HG, COL_DQ, COL_DK, COL_DV = range(8)
GROUP = 512

TM_PROJ = 512
TM_FFN = 256
FFN_CHUNK = 256
HG_CHUNK = 128
HG_TILE = 512
HG_SUB = SUBLANES
ATT_TQ = 256
ATT_TK = 512

NT_DIMS = (((1,), (1,)), ((), ()))
TN_DIMS = (((0,), (0,)), ((), ()))


def _silu(x):
    return x * jax.nn.sigmoid(x)


def _rms(x, w, eps):
    ms = jnp.mean(x * x, axis=-1, keepdims=True)
    return x * lax.rsqrt(ms + eps) * w


def _ada_kernel(c_ref, w_ref, b_ref, o_ref):
    s = _silu(c_ref[...]).astype(BF16)
    w = w_ref[0].astype(BF16)
    o_ref[0] = jnp.dot(s, w, preferred_element_type=F32) + b_ref[0]


def _ada_call(c_all, w_ada, b_ada):
    depth, d, n = w_ada.shape
    rows = c_all.shape[0]
    tn = 1536
    assert n % tn == 0
    return pl.pallas_call(
        _ada_kernel,
        out_shape=jax.ShapeDtypeStruct((depth, rows, n), F32),
        grid=(depth, n // tn),
        in_specs=[
            pl.BlockSpec((rows, d), lambda l, j: (0, 0)),
            pl.BlockSpec((1, d, tn), lambda l, j: (l, 0, j)),
            pl.BlockSpec((1, 1, tn), lambda l, j: (l, 0, j)),
        ],
        out_specs=pl.BlockSpec((1, rows, tn), lambda l, j: (l, 0, j)),
        compiler_params=pltpu.CompilerParams(
            dimension_semantics=("parallel", "parallel"), vmem_limit_bytes=VMEM_LIMIT_BYTES),
        name="adaln_mod",
    )(c_all, w_ada, b_ada.reshape(depth, 1, n))


def _inproj_kernel(x_ref, mod_ref, nw_ref, w_ref, cos_ref, sin_ref, o_ref):
    x = x_ref[0]
    h = _rms(x, nw_ref[...], NORM_EPS) * (1.0 + mod_ref[0, 1:2, :]) + mod_ref[0, 0:1, :]
    h = h.astype(BF16)
    cos = cos_ref[...]
    sin = sin_ref[...]
    lane = lax.broadcasted_iota(jnp.int32, cos.shape, 1)
    first_half = (lane % DA_HEAD_DIM) < (DA_HEAD_DIM // 2)
    n_groups = w_ref.shape[1] // GROUP
    for n in range(n_groups):
        y = jnp.dot(h, w_ref[:, n * GROUP:(n + 1) * GROUP], preferred_element_type=F32)
        if n in (COL_DQ, COL_DK):
            pieces = []
            for j in range(GROUP // LANES):
                yj = y[:, j * LANES:(j + 1) * LANES]
                partner = jnp.where(first_half,
                                    pltpu.roll(yj, LANES - DA_HEAD_DIM // 2, 1),
                                    pltpu.roll(yj, DA_HEAD_DIM // 2, 1))
                r = yj * cos + partner * sin
                if n == COL_DQ:
                    r = r * (DA_HEAD_DIM ** -0.5)
                pieces.append(r)
            y = jnp.concatenate(pieces, axis=1)
        o_ref[0, :, n * GROUP:(n + 1) * GROUP] = y.astype(BF16)


def _inproj_call(x, mod, nw, w_bf16, cos, sin):
    b, t, d = x.shape
    n = w_bf16.shape[1]
    tm = min(TM_PROJ, t)
    assert t % tm == 0
    return pl.pallas_call(
        _inproj_kernel,
        out_shape=jax.ShapeDtypeStruct((b, t, n), BF16),
        grid=(b, t // tm),
        in_specs=[
            pl.BlockSpec((1, tm, d), lambda i, j: (i, j, 0)),
            pl.BlockSpec((1, N_MOD, d), lambda i, j: (i, 0, 0)),
            pl.BlockSpec((1, d), lambda i, j: (0, 0)),
            pl.BlockSpec((d, n), lambda i, j: (0, 0), pipeline_mode=pl.Buffered(1)),
            pl.BlockSpec((tm, LANES), lambda i, j: (j, 0)),
            pl.BlockSpec((tm, LANES), lambda i, j: (j, 0)),
        ],
        out_specs=pl.BlockSpec((1, tm, n), lambda i, j: (i, j, 0)),
        compiler_params=pltpu.CompilerParams(
            dimension_semantics=("parallel", "parallel"), vmem_limit_bytes=VMEM_LIMIT_BYTES),
        name="norm_inproj",
    )(x, mod, nw.reshape(1, d), w_bf16, cos, sin)


def _hgrn_levels(ch):
    levels = []
    m = HG_SUB
    while m < ch:
        levels.append(m)
        m *= 2
    return levels


@functools.lru_cache(maxsize=None)
def _hgrn_constants(ch, reverse):
    idx = np.arange(ch)
    if reverse:
        cum = (idx[None, :] >= idx[:, None]).astype(np.float32)
    else:
        cum = (idx[None, :] <= idx[:, None]).astype(np.float32)
    blocks = [cum]
    masks = []
    for m in _hgrn_levels(ch):
        pair = idx // (2 * m)
        in_second = (idx % (2 * m)) >= m
        ref = pair * 2 * m + (m if reverse else m - 1)
        blocks.append(cum - cum[ref])
        later = ~in_second if reverse else in_second
        earlier = ~later
        masks.append(((pair[:, None] == pair[None, :]) & later[:, None] & earlier[None, :]).astype(np.float32))
    blocks.append(1.0 - cum)
    return np.concatenate(blocks, axis=0), np.stack(masks, axis=0)


def _hgrn_kernel(*refs, reverse, final, ch, n_chunks):
    if final:
        (hq_ref, hf_ref, hi_ref, lb_ref, w_ref, mask_ref, of_ref, hg_ref, gw_ref, o_ref, st_ref) = refs
    else:
        (hq_ref, hf_ref, hi_ref, lb_ref, w_ref, mask_ref, o_ref, st_ref) = refs
    levels = _hgrn_levels(ch)
    nsb = ch // HG_SUB

    @pl.when(pl.program_id(1) == 0)
    def _():
        st_ref[...] = jnp.zeros_like(st_ref)

    lb = lb_ref[...]
    log_lb = jnp.log(jnp.maximum(lb, LOG_FLOOR))
    log_1m_lb = jnp.log1p(-lb)
    one_m_lb = 1.0 - lb
    ones = jnp.ones((LANES, LANES), BF16)
    t_iota = lax.broadcasted_iota(jnp.int32, (nsb, HG_SUB, HG_DK), 1)

    def chunk_body(i, carry):
        c = (n_chunks - 1 - i) if reverse else i
        rows = pl.ds(pl.multiple_of(c * ch, ch), ch)
        for h in range(HG_HEADS):
            cols = slice(h * HG_DK, (h + 1) * HG_DK)
            hq = hq_ref[0, rows, cols].astype(F32)
            z = hf_ref[0, rows, cols].astype(F32)
            v = hi_ref[0, rows, cols]
            q = _silu(hq)
            e = jnp.exp(-jnp.abs(z))
            log_sig = jnp.minimum(z, 0.0) - jnp.log1p(e)
            a0 = log_lb[:, cols]
            a1 = log_1m_lb[:, cols] + log_sig
            g = jnp.maximum(a0, a1) + jnp.log1p(jnp.exp(-jnp.abs(a0 - a1)))
            kk = one_m_lb[:, cols] * (jnp.where(z >= 0.0, e, 1.0) / (1.0 + e))

            g_hi = g.astype(BF16)
            r1 = g - g_hi.astype(F32)
            g_mid = r1.astype(BF16)
            g_lo = (r1 - g_mid.astype(F32)).astype(BF16)
            g3 = jnp.concatenate([g_hi, g_mid, g_lo], axis=1)
            y = jnp.dot(w_ref[...], g3, preferred_element_type=F32)
            y = y[:, 0:HG_DK] + y[:, HG_DK:2 * HG_DK] + y[:, 2 * HG_DK:3 * HG_DK]
            b = y[0:ch]
            e_end = y[(1 + len(levels)) * ch:(2 + len(levels)) * ch]
            b_end = b[0:1] if reverse else b[ch - 1:ch]

            qs = (q * jnp.exp(b)).astype(BF16)
            ks = (kk * jnp.exp(e_end)).astype(BF16)

            a = None
            for l in range(len(levels)):
                d = y[(1 + l) * ch:(2 + l) * ch]
                el = jnp.exp(-jnp.abs(d))
                ql = (q * el).astype(BF16)
                kl = (kk * el).astype(BF16)
                al = lax.dot_general(ql, kl, NT_DIMS, preferred_element_type=F32) * mask_ref[l]
                a = al if a is None else a + al

            q3 = q.reshape(nsb, HG_SUB, HG_DK)
            k3 = kk.reshape(nsb, HG_SUB, HG_DK)
            b3 = b.reshape(nsb, HG_SUB, HG_DK)
            v3 = v.astype(F32).reshape(nsb, HG_SUB, HG_DK)
            ps = []
            for s in range(HG_SUB):
                dd = jnp.minimum(b3 - b3[:, s:s + 1, :], 0.0)
                p = q3 * k3[:, s:s + 1, :] * jnp.exp(dd)
                valid = (t_iota <= s) if reverse else (t_iota >= s)
                ps.append(jnp.where(valid, p, 0.0).reshape(ch, HG_DK))
            pcat = jnp.concatenate(ps, axis=0).astype(BF16)
            r = jnp.dot(pcat, ones, preferred_element_type=F32)
            od = None
            for s in range(HG_SUB):
                t = r[s * ch:(s + 1) * ch].reshape(nsb, HG_SUB, HG_DK) * v3[:, s:s + 1, :]
                od = t if od is None else od + t
            od = od.reshape(ch, HG_DK)

            st = st_ref[h]
            o = (jnp.dot(a.astype(BF16), v, preferred_element_type=F32) + od
                 + lax.dot_general(qs, st.astype(BF16), NT_DIMS, preferred_element_type=F32))
            st_ref[h] = st * jnp.exp(b_end) + lax.dot_general(v, ks, TN_DIMS, preferred_element_type=F32)

            if final:
                o = o + of_ref[0, rows, cols]
                o = _rms(o, gw_ref[...], NORM_EPS) * _silu(hg_ref[0, rows, cols].astype(F32))
                o_ref[0, rows, cols] = o.astype(o_ref.dtype)
            else:
                o_ref[0, rows, cols] = o
        return carry

    lax.fori_loop(0, n_chunks, chunk_body, 0)


def _hgrn_call(proj, lb_dir, gnorm_w, o_fwd, *, reverse):
    b, t, _ = proj.shape
    final = o_fwd is not None
    ch = min(HG_CHUNK, t)
    tile = min(HG_TILE, t)
    assert t % tile == 0 and tile % ch == 0
    nt = t // tile
    w_np, mask_np = _hgrn_constants(ch, reverse)
    w_c = jnp.asarray(w_np, BF16)
    mask_c = jnp.asarray(mask_np, F32)

    def tok(j):
        return (nt - 1 - j) if reverse else j

    def col_spec(group):
        return pl.BlockSpec((1, tile, GROUP), lambda i, j: (i, tok(j), group))

    in_specs = [
        col_spec(COL_HQ),
        col_spec(COL_HFB if reverse else COL_HFF),
        col_spec(COL_HI),
        pl.BlockSpec((1, HG_WIDTH), lambda i, j: (0, 0)),
        pl.BlockSpec(w_np.shape, lambda i, j: (0, 0)),
        pl.BlockSpec(mask_np.shape, lambda i, j: (0, 0, 0)),
    ]
    args = [proj, proj, proj, lb_dir.reshape(1, HG_WIDTH), w_c, mask_c]
    if final:
        in_specs += [
            pl.BlockSpec((1, tile, HG_WIDTH), lambda i, j: (i, tok(j), 0)),
            col_spec(COL_HG),
            pl.BlockSpec((1, HG_DK), lambda i, j: (0, 0)),
        ]
        args += [o_fwd, proj, gnorm_w.reshape(1, HG_DK)]
    kern = functools.partial(_hgrn_kernel, reverse=reverse, final=final, ch=ch, n_chunks=tile // ch)
    return pl.pallas_call(
        kern,
        out_shape=jax.ShapeDtypeStruct((b, t, HG_WIDTH), BF16 if final else F32),
        grid=(b, nt),
        in_specs=in_specs,
        out_specs=pl.BlockSpec((1, tile, HG_WIDTH), lambda i, j: (i, tok(j), 0)),
        scratch_shapes=[pltpu.VMEM((HG_HEADS, HG_DK, HG_DK), F32)],
        compiler_params=pltpu.CompilerParams(
            dimension_semantics=("parallel", "arbitrary"), vmem_limit_bytes=VMEM_LIMIT_BYTES),
        name="hgrn2_bwd" if reverse else "hgrn2_fwd",
    )(*args)


def _attn_kernel(q_ref, k_ref, v_ref, lq1_ref, lk1_ref, lq2_ref, lk2_ref, sw_ref, o_ref,
                 qs_ref, m_ref, l_ref, acc_ref, *, tq, lambda_init):
    kv = pl.program_id(3)

    @pl.when(kv == 0)
    def _():
        q = q_ref[0]
        lane = lax.broadcasted_iota(jnp.int32, q.shape, 1)
        zero = jnp.zeros_like(q)
        qs_ref[0:tq, :] = jnp.where(lane < DA_HEAD_DIM, q, zero)
        qs_ref[tq:2 * tq, :] = jnp.where(lane >= DA_HEAD_DIM, q, zero)
        m_ref[...] = jnp.full_like(m_ref, -jnp.inf)
        l_ref[...] = jnp.zeros_like(l_ref)
        acc_ref[...] = jnp.zeros_like(acc_ref)

    k = k_ref[0]
    v = v_ref[0]
    s = lax.dot_general(qs_ref[...], k, NT_DIMS, preferred_element_type=F32)
    m_prev = m_ref[...]
    m_new = jnp.maximum(m_prev, jnp.max(s, axis=-1, keepdims=True))
    alpha = jnp.exp(m_prev - m_new)
    p = jnp.exp(s - m_new)
    l_ref[...] = alpha * l_ref[...] + jnp.sum(p, axis=-1, keepdims=True)
    acc_ref[...] = alpha * acc_ref[...] + jnp.dot(p.astype(BF16), v, preferred_element_type=F32)
    m_ref[...] = m_new

    @pl.when(kv == pl.num_programs(3) - 1)
    def _():
        o = acc_ref[...] / l_ref[...]
        lam = (jnp.exp(jnp.sum(lq1_ref[...] * lk1_ref[...], axis=-1, keepdims=True))
               - jnp.exp(jnp.sum(lq2_ref[...] * lk2_ref[...], axis=-1, keepdims=True)) + lambda_init)
        od = o[0:tq] - lam * o[tq:2 * tq]
        od = _rms(od, sw_ref[...], SUBLN_EPS) * (1.0 - lambda_init)
        o_ref[0] = od.astype(o_ref.dtype)


def _attn_call(proj, lq1, lk1, lq2, lk2, subln_w, lambda_init):
    b, t, _ = proj.shape
    tq = min(ATT_TQ, t)
    tk = min(ATT_TK, t)
    assert t % tq == 0 and t % tk == 0
    head_w = 2 * DA_HEAD_DIM
    blocks_per_group = GROUP // head_w

    def lam_spec():
        return pl.BlockSpec((1, DA_HEAD_DIM), lambda i, h, qi, ki: (0, 0))

    kern = functools.partial(_attn_kernel, tq=tq, lambda_init=lambda_init)
    return pl.pallas_call(
        kern,
        out_shape=jax.ShapeDtypeStruct((b, t, DA_WIDTH), BF16),
        grid=(b, DA_HEADS, t // tq, t // tk),
        in_specs=[
            pl.BlockSpec((1, tq, head_w), lambda i, h, qi, ki: (i, qi, COL_DQ * blocks_per_group + h)),
            pl.BlockSpec((1, tk, head_w), lambda i, h, qi, ki: (i, ki, COL_DK * blocks_per_group + h)),
            pl.BlockSpec((1, tk, head_w), lambda i, h, qi, ki: (i, ki, COL_DV * blocks_per_group + h)),
            lam_spec(), lam_spec(), lam_spec(), lam_spec(),
            pl.BlockSpec((1, head_w), lambda i, h, qi, ki: (0, 0)),
        ],
        out_specs=pl.BlockSpec((1, tq, head_w), lambda i, h, qi, ki: (i, qi, h)),
        scratch_shapes=[
            pltpu.VMEM((2 * tq, head_w), BF16),
            pltpu.VMEM((2 * tq, 1), F32),
            pltpu.VMEM((2 * tq, 1), F32),
            pltpu.VMEM((2 * tq, head_w), F32),
        ],
        compiler_params=pltpu.CompilerParams(
            dimension_semantics=("parallel", "parallel", "parallel", "arbitrary"),
            vmem_limit_bytes=VMEM_LIMIT_BYTES),
        name="diff_attn",
    )(proj, proj, proj, lq1.reshape(1, -1), lk1.reshape(1, -1), lq2.reshape(1, -1), lk2.reshape(1, -1),
      subln_w.reshape(1, -1))


def _outffn_kernel(x_ref, ohg_ref, oda_ref, mod_ref, nw_ref, wo_ref, wg_ref, wu_ref, wd_ref, o_ref):
    x = x_ref[0]
    mix = (jnp.dot(ohg_ref[0], wo_ref[0:HG_WIDTH, :], preferred_element_type=F32)
           + jnp.dot(oda_ref[0], wo_ref[HG_WIDTH:HG_WIDTH + DA_WIDTH, :], preferred_element_type=F32))
    x1 = x + mod_ref[0, 2:3, :] * _rms(mix, nw_ref[0:1, :], NORM_EPS)
    h = _rms(x1, nw_ref[1:2, :], NORM_EPS) * (1.0 + mod_ref[0, 4:5, :]) + mod_ref[0, 3:4, :]
    h = h.astype(BF16)
    hidden = wg_ref.shape[1]
    f = None
    for c in range(hidden // FFN_CHUNK):
        cs = slice(c * FFN_CHUNK, (c + 1) * FFN_CHUNK)
        g = jnp.dot(h, wg_ref[:, cs], preferred_element_type=F32)
        u = jnp.dot(h, wu_ref[:, cs], preferred_element_type=F32)
        a = (_silu(g) * u).astype(BF16)
        fc = jnp.dot(a, wd_ref[cs, :], preferred_element_type=F32)
        f = fc if f is None else f + fc
    o_ref[0] = x1 + mod_ref[0, 5:6, :] * _rms(f, nw_ref[2:3, :], NORM_EPS)


def _outffn_call(x, o_hg, o_da, mod, norms, wo, wg, wu, wd):
    b, t, d = x.shape
    hidden = wg.shape[1]
    tm = min(TM_FFN, t)
    assert t % tm == 0 and hidden % FFN_CHUNK == 0

    def resident(shape):
        return pl.BlockSpec(shape, lambda i, j: (0, 0), pipeline_mode=pl.Buffered(1))

    return pl.pallas_call(
        _outffn_kernel,
        out_shape=jax.ShapeDtypeStruct((b, t, d), F32),
        grid=(b, t // tm),
        in_specs=[
            pl.BlockSpec((1, tm, d), lambda i, j: (i, j, 0)),
            pl.BlockSpec((1, tm, HG_WIDTH), lambda i, j: (i, j, 0)),
            pl.BlockSpec((1, tm, DA_WIDTH), lambda i, j: (i, j, 0)),
            pl.BlockSpec((1, N_MOD, d), lambda i, j: (i, 0, 0)),
            pl.BlockSpec((3, d), lambda i, j: (0, 0)),
            resident((HG_WIDTH + DA_WIDTH, d)),
            resident((d, hidden)),
            resident((d, hidden)),
            resident((hidden, d)),
        ],
        out_specs=pl.BlockSpec((1, tm, d), lambda i, j: (i, j, 0)),
        compiler_params=pltpu.CompilerParams(
            dimension_semantics=("parallel", "parallel"), vmem_limit_bytes=VMEM_LIMIT_BYTES),
        name="outproj_ffn",
    )(x, o_hg, o_da, mod, norms, wo, wg, wu, wd)


def _rotary_tables(t):
    dh = DA_HEAD_DIM
    inv = 1.0 / (ROPE_THETA ** (jnp.arange(0, dh, 2, dtype=F32) / dh))
    ang = jnp.arange(t, dtype=F32)[:, None] * inv[None, :]
    cos = jnp.cos(ang)
    sin = jnp.sin(ang)
    reps = LANES // dh
    cos_t = jnp.tile(jnp.concatenate([cos, cos], axis=-1), (1, reps))
    sin_t = jnp.tile(jnp.concatenate([-sin, sin], axis=-1), (1, reps))
    return cos_t, sin_t


def kernel(x_prompt, x_sample, c_prompt, c_sample, w_ada, b_ada, norm_pre_mix, norm_post_mix, norm_pre_ffn, norm_post_ffn, w_in, hg_lower_bounds, hg_gnorm, da_lambda_q1, da_lambda_k1, da_lambda_q2, da_lambda_k2, da_subln, w_out, w_ffn_gate, w_ffn_up, w_ffn_down):
    depth = w_in.shape[0]
    d = x_prompt.shape[-1]

    lb_soft = jax.nn.softmax(hg_lower_bounds.astype(F32), axis=1)
    lb_all = jnp.cumsum(lb_soft, axis=1) - lb_soft[:, :1]

    c_all = jnp.concatenate([c_prompt, c_sample], axis=0)
    mod_all = _ada_call(c_all, w_ada, b_ada)

    w_in_b = w_in.astype(BF16)
    w_out_b = w_out.astype(BF16)
    wg_b = w_ffn_gate.astype(BF16)
    wu_b = w_ffn_up.astype(BF16)
    wd_b = w_ffn_down.astype(BF16)

    def trunk(x, row0):
        b, t, _ = x.shape
        cos, sin = _rotary_tables(t)
        for l in range(depth):
            mod = mod_all[l, row0:row0 + b].reshape(b, N_MOD, d)
            lambda_init = 0.8 - 0.6 * math.exp(-0.3 * l)
            proj = _inproj_call(x, mod, norm_pre_mix[l], w_in_b[l], cos, sin)
            o_f = _hgrn_call(proj, lb_all[0, l], None, None, reverse=False)
            o_hg = _hgrn_call(proj, lb_all[1, l], hg_gnorm[l], o_f, reverse=True)
            o_da = _attn_call(proj, da_lambda_q1[l], da_lambda_k1[l], da_lambda_q2[l], da_lambda_k2[l],
                              da_subln[l], lambda_init)
            norms = jnp.stack([norm_post_mix[l], norm_pre_ffn[l], norm_post_ffn[l]], axis=0)
            x = _outffn_call(x, o_hg, o_da, mod, norms, w_out_b[l], wg_b[l], wu_b[l], wd_b[l])
        return x

    y_prompt = trunk(x_prompt, 0)
    y_sample = trunk(x_sample, c_prompt.shape[0])
    return (y_prompt, y_sample)
```

```python
import functools
import math

import numpy as np
import jax
import jax.numpy as jnp
from jax import lax
from jax.experimental import pallas as pl
from jax.experimental.pallas import tpu as pltpu

F32 = jnp.float32
BF16 = jnp.bfloat16

HG_WIDTH = 512
HG_HEADS = 4
HG_DK = 128
DA_WIDTH = 512
DA_HEADS = 4
DA_HEAD_DIM = 64
ROPE_THETA = 10000.0
NORM_EPS = 1e-6
SUBLN_EPS = 1e-5
LOG_FLOOR = 1e-30
LOG2_E = math.log2(math.e)
N_MOD = 6

LANES = 128
SUBLANES = 8
VMEM_LIMIT_BYTES = 56 * 1024 * 1024

COL_HQ, COL_HFF, COL_HFB, COL_HI, COL_HG, COL_DQ, COL_DK, COL_DV = range(8)
GROUP = 512

TM_PROJ = 512
TM_FFN = 256
FFN_CHUNK = 256
HG_CHUNK = 128
HG_TILE = 512
HG_SUB = SUBLANES
ATT_TQ = 256
ATT_TK = 512

NT_DIMS = (((1,), (1,)), ((), ()))
TN_DIMS = (((0,), (0,)), ((), ()))


def _silu(x):
    return x * jax.nn.sigmoid(x)


def _rms(x, w, eps):
    ms = jnp.mean(x * x, axis=-1, keepdims=True)
    return x * lax.rsqrt(ms + eps) * w


def _ada_kernel(c_ref, w_ref, b_ref, o_ref):
    s = _silu(c_ref[...]).astype(BF16)
    w = w_ref[0].astype(BF16)
    o_ref[0] = jnp.dot(s, w, preferred_element_type=F32) + b_ref[0]


def _ada_call(c_all, w_ada, b_ada):
    depth, d, n = w_ada.shape
    rows = c_all.shape[0]
    tn = 1536
    assert n % tn == 0
    return pl.pallas_call(
        _ada_kernel,
        out_shape=jax.ShapeDtypeStruct((depth, rows, n), F32),
        grid=(depth, n // tn),
        in_specs=[
            pl.BlockSpec((rows, d), lambda l, j: (0, 0)),
            pl.BlockSpec((1, d, tn), lambda l, j: (l, 0, j)),
            pl.BlockSpec((1, 1, tn), lambda l, j: (l, 0, j)),
        ],
        out_specs=pl.BlockSpec((1, rows, tn), lambda l, j: (l, 0, j)),
        compiler_params=pltpu.CompilerParams(
            dimension_semantics=("parallel", "parallel"), vmem_limit_bytes=VMEM_LIMIT_BYTES),
        name="adaln_mod",
    )(c_all, w_ada, b_ada.reshape(depth, 1, n))


def _inproj_kernel(x_ref, mod_ref, nw_ref, w_ref, cos_ref, sin_ref, o_ref):
    x = x_ref[0]
    h = _rms(x, nw_ref[...], NORM_EPS) * (1.0 + mod_ref[0, 1:2, :]) + mod_ref[0, 0:1, :]
    h = h.astype(BF16)
    cos = cos_ref[...]
    sin = sin_ref[...]
    lane = lax.broadcasted_iota(jnp.int32, cos.shape, 1)
    first_half = (lane % DA_HEAD_DIM) < (DA_HEAD_DIM // 2)
    n_groups = w_ref.shape[1] // GROUP
    for n in range(n_groups):
        y = jnp.dot(h, w_ref[:, n * GROUP:(n + 1) * GROUP], preferred_element_type=F32)
        if n in (COL_DQ, COL_DK):
            pieces = []
            for j in range(GROUP // LANES):
                yj = y[:, j * LANES:(j + 1) * LANES]
                partner = jnp.where(first_half,
                                    pltpu.roll(yj, LANES - DA_HEAD_DIM // 2, 1),
                                    pltpu.roll(yj, DA_HEAD_DIM // 2, 1))
                r = yj * cos + partner * sin
                if n == COL_DQ:
                    r = r * (DA_HEAD_DIM ** -0.5 * LOG2_E)
                pieces.append(r)
            y = jnp.concatenate(pieces, axis=1)
        o_ref[0, :, n * GROUP:(n + 1) * GROUP] = y.astype(BF16)


def _inproj_call(x, mod, nw, w_bf16, cos, sin):
    b, t, d = x.shape
    n = w_bf16.shape[1]
    tm = min(TM_PROJ, t)
    assert t % tm == 0
    return pl.pallas_call(
        _inproj_kernel,
        out_shape=jax.ShapeDtypeStruct((b, t, n), BF16),
        grid=(b, t // tm),
        in_specs=[
            pl.BlockSpec((1, tm, d), lambda i, j: (i, j, 0)),
            pl.BlockSpec((1, N_MOD, d), lambda i, j: (i, 0, 0)),
            pl.BlockSpec((1, d), lambda i, j: (0, 0)),
            pl.BlockSpec((d, n), lambda i, j: (0, 0), pipeline_mode=pl.Buffered(1)),
            pl.BlockSpec((tm, LANES), lambda i, j: (j, 0)),
            pl.BlockSpec((tm, LANES), lambda i, j: (j, 0)),
        ],
        out_specs=pl.BlockSpec((1, tm, n), lambda i, j: (i, j, 0)),
        compiler_params=pltpu.CompilerParams(
            dimension_semantics=("parallel", "parallel"), vmem_limit_bytes=VMEM_LIMIT_BYTES),
        name="norm_inproj",
    )(x, mod, nw.reshape(1, d), w_bf16, cos, sin)


def _hgrn_levels(ch):
    levels = []
    m = HG_SUB
    while m < ch:
        levels.append(m)
        m *= 2
    return levels


@functools.lru_cache(maxsize=None)
def _hgrn_constants(ch, reverse):
    idx = np.arange(ch)
    if reverse:
        cum = (idx[None, :] >= idx[:, None]).astype(np.float32)
    else:
        cum = (idx[None, :] <= idx[:, None]).astype(np.float32)
    blocks = [cum]
    masks = []
    for m in _hgrn_levels(ch):
        pair = idx // (2 * m)
        in_second = (idx % (2 * m)) >= m
        ref = pair * 2 * m + (m if reverse else m - 1)
        blocks.append(cum - cum[ref])
        later = ~in_second if reverse else in_second
        earlier = ~later
        masks.append(((pair[:, None] == pair[None, :]) & later[:, None] & earlier[None, :]).astype(np.float32))
    blocks.append(1.0 - cum)
    return np.concatenate(blocks, axis=0), np.stack(masks, axis=0)


def _hgrn_kernel(*refs, reverse, final, ch, n_chunks):
    if final:
        (hq_ref, hf_ref, hi_ref, lb_ref, w_ref, mask_ref, of_ref, hg_ref, gw_ref, o_ref, st_ref) = refs
    else:
        (hq_ref, hf_ref, hi_ref, lb_ref, w_ref, mask_ref, o_ref, st_ref) = refs
    levels = _hgrn_levels(ch)
    nsb = ch // HG_SUB

    @pl.when(pl.program_id(1) == 0)
    def _():
        st_ref[...] = jnp.zeros_like(st_ref)

    lb = lb_ref[...]
    log_lb = jnp.log(jnp.maximum(lb, LOG_FLOOR))
    log_1m_lb = jnp.log1p(-lb)
    one_m_lb = 1.0 - lb
    ones = jnp.ones((LANES, LANES), BF16)
    t_iota = lax.broadcasted_iota(jnp.int32, (nsb, HG_SUB, HG_DK), 1)

    def chunk_body(i, carry):
        c = (n_chunks - 1 - i) if reverse else i
        rows = pl.ds(pl.multiple_of(c * ch, ch), ch)
        for h in range(HG_HEADS):
            cols = slice(h * HG_DK, (h + 1) * HG_DK)
            hq = hq_ref[0, rows, cols].astype(F32)
            z = hf_ref[0, rows, cols].astype(F32)
            v = hi_ref[0, rows, cols]
            q = _silu(hq)
            e = jnp.exp(-jnp.abs(z))
            log_sig = jnp.minimum(z, 0.0) - jnp.log1p(e)
            a0 = log_lb[:, cols]
            a1 = log_1m_lb[:, cols] + log_sig
            g = jnp.maximum(a0, a1) + jnp.log1p(jnp.exp(-jnp.abs(a0 - a1)))
            kk = one_m_lb[:, cols] * (jnp.where(z >= 0.0, e, 1.0) / (1.0 + e))

            g_hi = g.astype(BF16)
            r1 = g - g_hi.astype(F32)
            g_mid = r1.astype(BF16)
            g_lo = (r1 - g_mid.astype(F32)).astype(BF16)
            g3 = jnp.concatenate([g_hi, g_mid, g_lo], axis=1)
            y = jnp.dot(w_ref[...], g3, preferred_element_type=F32)
            y = y[:, 0:HG_DK] + y[:, HG_DK:2 * HG_DK] + y[:, 2 * HG_DK:3 * HG_DK]
            b = y[0:ch]
            e_end = y[(1 + len(levels)) * ch:(2 + len(levels)) * ch]
            b_end = b[0:1] if reverse else b[ch - 1:ch]

            qs = (q * jnp.exp(b)).astype(BF16)
            ks = (kk * jnp.exp(e_end)).astype(BF16)

            a = None
            for l in range(len(levels)):
                d = y[(1 + l) * ch:(2 + l) * ch]
                el = jnp.exp(-jnp.abs(d))
                ql = (q * el).astype(BF16)
                kl = (kk * el).astype(BF16)
                al = lax.dot_general(ql, kl, NT_DIMS, preferred_element_type=F32) * mask_ref[l]
                a = al if a is None else a + al

            q3 = q.reshape(nsb, HG_SUB, HG_DK)
            k3 = kk.reshape(nsb, HG_SUB, HG_DK)
            b3 = b.reshape(nsb, HG_SUB, HG_DK)
            v3 = v.astype(F32).reshape(nsb, HG_SUB, HG_DK)
            ps = []
            for s in range(HG_SUB):
                dd = jnp.minimum(b3 - b3[:, s:s + 1, :], 0.0)
                p = q3 * k3[:, s:s + 1, :] * jnp.exp(dd)
                valid = (t_iota <= s) if reverse else (t_iota >= s)
                ps.append(jnp.where(valid, p, 0.0).reshape(ch, HG_DK))
            pcat = jnp.concatenate(ps, axis=0).astype(BF16)
            r = jnp.dot(pcat, ones, preferred_element_type=F32)
            od = None
            for s in range(HG_SUB):
                t = r[s * ch:(s + 1) * ch].reshape(nsb, HG_SUB, HG_DK) * v3[:, s:s + 1, :]
                od = t if od is None else od + t
            od = od.reshape(ch, HG_DK)

            st = st_ref[h]
            o = (jnp.dot(a.astype(BF16), v, preferred_element_type=F32) + od
                 + lax.dot_general(qs, st.astype(BF16), NT_DIMS, preferred_element_type=F32))
            st_ref[h] = st * jnp.exp(b_end) + lax.dot_general(v, ks, TN_DIMS, preferred_element_type=F32)

            if final:
                o = o + of_ref[0, rows, cols]
                o = _rms(o, gw_ref[...], NORM_EPS) * _silu(hg_ref[0, rows, cols].astype(F32))
                o_ref[0, rows, cols] = o.astype(o_ref.dtype)
            else:
                o_ref[0, rows, cols] = o
        return carry

    lax.fori_loop(0, n_chunks, chunk_body, 0)


def _hgrn_call(proj, lb_dir, gnorm_w, o_fwd, *, reverse):
    b, t, _ = proj.shape
    final = o_fwd is not None
    ch = min(HG_CHUNK, t)
    tile = min(HG_TILE, t)
    assert t % tile == 0 and tile % ch == 0
    nt = t // tile
    w_np, mask_np = _hgrn_constants(ch, reverse)
    w_c = jnp.asarray(w_np, BF16)
    mask_c = jnp.asarray(mask_np, F32)

    def tok(j):
        return (nt - 1 - j) if reverse else j

    def col_spec(group):
        return pl.BlockSpec((1, tile, GROUP), lambda i, j: (i, tok(j), group))

    in_specs = [
        col_spec(COL_HQ),
        col_spec(COL_HFB if reverse else COL_HFF),
        col_spec(COL_HI),
        pl.BlockSpec((1, HG_WIDTH), lambda i, j: (0, 0)),
        pl.BlockSpec(w_np.shape, lambda i, j: (0, 0)),
        pl.BlockSpec(mask_np.shape, lambda i, j: (0, 0, 0)),
    ]
    args = [proj, proj, proj, lb_dir.reshape(1, HG_WIDTH), w_c, mask_c]
    if final:
        in_specs += [
            pl.BlockSpec((1, tile, HG_WIDTH), lambda i, j: (i, tok(j), 0)),
            col_spec(COL_HG),
            pl.BlockSpec((1, HG_DK), lambda i, j: (0, 0)),
        ]
        args += [o_fwd, proj, gnorm_w.reshape(1, HG_DK)]
    kern = functools.partial(_hgrn_kernel, reverse=reverse, final=final, ch=ch, n_chunks=tile // ch)
    return pl.pallas_call(
        kern,
        out_shape=jax.ShapeDtypeStruct((b, t, HG_WIDTH), BF16 if final else F32),
        grid=(b, nt),
        in_specs=in_specs,
        out_specs=pl.BlockSpec((1, tile, HG_WIDTH), lambda i, j: (i, tok(j), 0)),
        scratch_shapes=[pltpu.VMEM((HG_HEADS, HG_DK, HG_DK), F32)],
        compiler_params=pltpu.CompilerParams(
            dimension_semantics=("parallel", "arbitrary"), vmem_limit_bytes=VMEM_LIMIT_BYTES),
        name="hgrn2_bwd" if reverse else "hgrn2_fwd",
    )(*args)


def _attn_kernel(q_ref, k_ref, v_ref, lq1_ref, lk1_ref, lq2_ref, lk2_ref, sw_ref, o_ref,
                 *, tq, tkc, n_kv, lambda_init):
    q = q_ref[0]
    lane = lax.broadcasted_iota(jnp.int32, q.shape, 1)
    zero = jnp.zeros_like(q)
    qs = jnp.concatenate([jnp.where(lane < DA_HEAD_DIM, q, zero),
                          jnp.where(lane >= DA_HEAD_DIM, q, zero)], axis=0)

    def body(j, carry):
        m_prev, l_prev, acc = carry
        rows = pl.ds(pl.multiple_of(j * tkc, tkc), tkc)
        k = k_ref[0, rows, :]
        v = v_ref[0, rows, :]
        s = lax.dot_general(qs, k, NT_DIMS, preferred_element_type=F32)
        m_new = jnp.maximum(m_prev, jnp.max(s, axis=-1, keepdims=True))
        alpha = jnp.exp2(m_prev - m_new)
        p = jnp.exp2(s - m_new)
        l_new = alpha * l_prev + jnp.sum(p, axis=-1, keepdims=True)
        acc = alpha * acc + jnp.dot(p.astype(BF16), v, preferred_element_type=F32)
        return m_new, l_new, acc

    init = (jnp.full((2 * tq, 1), -jnp.inf, F32), jnp.zeros((2 * tq, 1), F32),
            jnp.zeros((2 * tq, 2 * DA_HEAD_DIM), F32))
    _, l_fin, acc = lax.fori_loop(0, n_kv, body, init, unroll=True)

    o = acc / l_fin
    lam = (jnp.exp(jnp.sum(lq1_ref[...] * lk1_ref[...], axis=-1, keepdims=True))
           - jnp.exp(jnp.sum(lq2_ref[...] * lk2_ref[...], axis=-1, keepdims=True)) + lambda_init)
    od = o[0:tq] - lam * o[tq:2 * tq]
    od = _rms(od, sw_ref[...], SUBLN_EPS) * (1.0 - lambda_init)
    o_ref[0] = od.astype(o_ref.dtype)


def _attn_call(proj, lq1, lk1, lq2, lk2, subln_w, lambda_init):
    b, t, _ = proj.shape
    tq = min(ATT_TQ, t)
    tkc = min(ATT_TK, t)
    assert t % tq == 0 and t % tkc == 0
    head_w = 2 * DA_HEAD_DIM
    blocks_per_group = GROUP // head_w

    def lam_spec():
        return pl.BlockSpec((1, DA_HEAD_DIM), lambda i, h, qi: (0, 0))

    kern = functools.partial(_attn_kernel, tq=tq, tkc=tkc, n_kv=t // tkc, lambda_init=lambda_init)
    return pl.pallas_call(
        kern,
        out_shape=jax.ShapeDtypeStruct((b, t, DA_WIDTH), BF16),
        grid=(b, DA_HEADS, t // tq),
        in_specs=[
            pl.BlockSpec((1, tq, head_w), lambda i, h, qi: (i, qi, COL_DQ * blocks_per_group + h)),
            pl.BlockSpec((1, t, head_w), lambda i, h, qi: (i, 0, COL_DK * blocks_per_group + h)),
            pl.BlockSpec((1, t, head_w), lambda i, h, qi: (i, 0, COL_DV * blocks_per_group + h)),
            lam_spec(), lam_spec(), lam_spec(), lam_spec(),
            pl.BlockSpec((1, head_w), lambda i, h, qi: (0, 0)),
        ],
        out_specs=pl.BlockSpec((1, tq, head_w), lambda i, h, qi: (i, qi, h)),
        compiler_params=pltpu.CompilerParams(
            dimension_semantics=("parallel", "parallel", "arbitrary"),
            vmem_limit_bytes=VMEM_LIMIT_BYTES),
        name="diff_attn",
    )(proj, proj, proj, lq1.reshape(1, -1), lk1.reshape(1, -1), lq2.reshape(1, -1), lk2.reshape(1, -1),
      subln_w.reshape(1, -1))


def _outffn_kernel(x_ref, ohg_ref, oda_ref, mod_ref, nw_ref, wo_ref, wg_ref, wu_ref, wd_ref, o_ref):
    x = x_ref[0]
    mix = (jnp.dot(ohg_ref[0], wo_ref[0:HG_WIDTH, :], preferred_element_type=F32)
           + jnp.dot(oda_ref[0], wo_ref[HG_WIDTH:HG_WIDTH + DA_WIDTH, :], preferred_element_type=F32))
    x1 = x + mod_ref[0, 2:3, :] * _rms(mix, nw_ref[0:1, :], NORM_EPS)
    h = _rms(x1, nw_ref[1:2, :], NORM_EPS) * (1.0 + mod_ref[0, 4:5, :]) + mod_ref[0, 3:4, :]
    h = h.astype(BF16)
    hidden = wg_ref.shape[1]
    f = None
    for c in range(hidden // FFN_CHUNK):
        cs = slice(c * FFN_CHUNK, (c + 1) * FFN_CHUNK)
        g = jnp.dot(h, wg_ref[:, cs], preferred_element_type=F32)
        u = jnp.dot(h, wu_ref[:, cs], preferred_element_type=F32)
        a = (_silu(g) * u).astype(BF16)
        fc = jnp.dot(a, wd_ref[cs, :], preferred_element_type=F32)
        f = fc if f is None else f + fc
    o_ref[0] = x1 + mod_ref[0, 5:6, :] * _rms(f, nw_ref[2:3, :], NORM_EPS)


def _outffn_call(x, o_hg, o_da, mod, norms, wo, wg, wu, wd):
    b, t, d = x.shape
    hidden = wg.shape[1]
    tm = min(TM_FFN, t)
    assert t % tm == 0 and hidden % FFN_CHUNK == 0

    def resident(shape):
        return pl.BlockSpec(shape, lambda i, j: (0, 0), pipeline_mode=pl.Buffered(1))

    return pl.pallas_call(
        _outffn_kernel,
        out_shape=jax.ShapeDtypeStruct((b, t, d), F32),
        grid=(b, t // tm),
        in_specs=[
            pl.BlockSpec((1, tm, d), lambda i, j: (i, j, 0)),
            pl.BlockSpec((1, tm, HG_WIDTH), lambda i, j: (i, j, 0)),
            pl.BlockSpec((1, tm, DA_WIDTH), lambda i, j: (i, j, 0)),
            pl.BlockSpec((1, N_MOD, d), lambda i, j: (i, 0, 0)),
            pl.BlockSpec((3, d), lambda i, j: (0, 0)),
            resident((HG_WIDTH + DA_WIDTH, d)),
            resident((d, hidden)),
            resident((d, hidden)),
            resident((hidden, d)),
        ],
        out_specs=pl.BlockSpec((1, tm, d), lambda i, j: (i, j, 0)),
        compiler_params=pltpu.CompilerParams(
            dimension_semantics=("parallel", "parallel"), vmem_limit_bytes=VMEM_LIMIT_BYTES),
        name="outproj_ffn",
    )(x, o_hg, o_da, mod, norms, wo, wg, wu, wd)


def _rotary_tables(t):
    dh = DA_HEAD_DIM
    inv = 1.0 / (ROPE_THETA ** (jnp.arange(0, dh, 2, dtype=F32) / dh))
    ang = jnp.arange(t, dtype=F32)[:, None] * inv[None, :]
    cos = jnp.cos(ang)
    sin = jnp.sin(ang)
    reps = LANES // dh
    cos_t = jnp.tile(jnp.concatenate([cos, cos], axis=-1), (1, reps))
    sin_t = jnp.tile(jnp.concatenate([-sin, sin], axis=-1), (1, reps))
    return cos_t, sin_t


def kernel(x_prompt, x_sample, c_prompt, c_sample, w_ada, b_ada, norm_pre_mix, norm_post_mix, norm_pre_ffn, norm_post_ffn, w_in, hg_lower_bounds, hg_gnorm, da_lambda_q1, da_lambda_k1, da_lambda_q2, da_lambda_k2, da_subln, w_out, w_ffn_gate, w_ffn_up, w_ffn_down):
    depth = w_in.shape[0]
    d = x_prompt.shape[-1]

    lb_soft = jax.nn.softmax(hg_lower_bounds.astype(F32), axis=1)
    lb_all = jnp.cumsum(lb_soft, axis=1) - lb_soft[:, :1]

    c_all = jnp.concatenate([c_prompt, c_sample], axis=0)
    mod_all = _ada_call(c_all, w_ada, b_ada)

    w_in_b = w_in.astype(BF16)
    w_out_b = w_out.astype(BF16)
    wg_b = w_ffn_gate.astype(BF16)
    wu_b = w_ffn_up.astype(BF16)
    wd_b = w_ffn_down.astype(BF16)

    def trunk(x, row0):
        b, t, _ = x.shape
        cos, sin = _rotary_tables(t)
        for l in range(depth):
            mod = mod_all[l, row0:row0 + b].reshape(b, N_MOD, d)
            lambda_init = 0.8 - 0.6 * math.exp(-0.3 * l)
            proj = _inproj_call(x, mod, norm_pre_mix[l], w_in_b[l], cos, sin)
            o_f = _hgrn_call(proj, lb_all[0, l], None, None, reverse=False)
            o_hg = _hgrn_call(proj, lb_all[1, l], hg_gnorm[l], o_f, reverse=True)
            o_da = _attn_call(proj, da_lambda_q1[l], da_lambda_k1[l], da_lambda_q2[l], da_lambda_k2[l],
                              da_subln[l], lambda_init)
            norms = jnp.stack([norm_post_mix[l], norm_pre_ffn[l], norm_post_ffn[l]], axis=0)
            x = _outffn_call(x, o_hg, o_da, mod, norms, w_out_b[l], wg_b[l], wu_b[l], wd_b[l])
        return x

    y_prompt = trunk(x_prompt, 0)
    y_sample = trunk(x_sample, c_prompt.shape[0])
    return (y_prompt, y_sample)
```

```python
import functools
import math

import numpy as np
import jax
import jax.numpy as jnp
from jax import lax
from jax.experimental import pallas as pl
from jax.experimental.pallas import tpu as pltpu

F32 = jnp.float32
BF16 = jnp.bfloat16

HG_WIDTH = 512
HG_HEADS = 4
HG_DK = 128
DA_WIDTH = 512
DA_HEADS = 4
DA_HEAD_DIM = 64
ROPE_THETA = 10000.0
NORM_EPS = 1e-6
SUBLN_EPS = 1e-5
LOG_FLOOR = 1e-30
LOG2_E = math.log2(math.e)
N_MOD = 6

LANES = 128
SUBLANES = 8
VMEM_LIMIT_BYTES = 56 * 1024 * 1024

COL_HQ, COL_HFF, COL_HFB, COL_HI, COL_HG, COL_DQ, COL_DK, COL_DV = range(8)
GROUP = 512

TM_PROJ = 512
TM_FFN = 256
FFN_CHUNK = 256
HG_CHUNK = 128
HG_TILE = 512
HG_SUB = SUBLANES
ATT_TQ = 256
ATT_TK = 512

NT_DIMS = (((1,), (1,)), ((), ()))
TN_DIMS = (((0,), (0,)), ((), ()))


def _silu(x):
    return x * jax.nn.sigmoid(x)


def _rms(x, w, eps):
    ms = jnp.mean(x * x, axis=-1, keepdims=True)
    return x * lax.rsqrt(ms + eps) * w


def _ada_kernel(c_ref, w_ref, b_ref, o_ref):
    s = _silu(c_ref[...]).astype(BF16)
    w = w_ref[0].astype(BF16)
    o_ref[0] = jnp.dot(s, w, preferred_element_type=F32) + b_ref[0]


def _ada_call(c_all, w_ada, b_ada):
    depth, d, n = w_ada.shape
    rows = c_all.shape[0]
    tn = 1536
    assert n % tn == 0
    return pl.pallas_call(
        _ada_kernel,
        out_shape=jax.ShapeDtypeStruct((depth, rows, n), F32),
        grid=(depth, n // tn),
        in_specs=[
            pl.BlockSpec((rows, d), lambda l, j: (0, 0)),
            pl.BlockSpec((1, d, tn), lambda l, j: (l, 0, j)),
            pl.BlockSpec((1, 1, tn), lambda l, j: (l, 0, j)),
        ],
        out_specs=pl.BlockSpec((1, rows, tn), lambda l, j: (l, 0, j)),
        compiler_params=pltpu.CompilerParams(
            dimension_semantics=("parallel", "parallel"), vmem_limit_bytes=VMEM_LIMIT_BYTES),
        name="adaln_mod",
    )(c_all, w_ada, b_ada.reshape(depth, 1, n))


def _inproj_kernel(x_ref, mod_ref, nw_ref, w_ref, cos_ref, sin_ref, o_ref):
    x = x_ref[0]
    h = _rms(x, nw_ref[...], NORM_EPS) * (1.0 + mod_ref[0, 1:2, :]) + mod_ref[0, 0:1, :]
    h = h.astype(BF16)
    cos = cos_ref[...]
    sin = sin_ref[...]
    lane = lax.broadcasted_iota(jnp.int32, cos.shape, 1)
    first_half = (lane % DA_HEAD_DIM) < (DA_HEAD_DIM // 2)
    n_groups = w_ref.shape[1] // GROUP
    for n in range(n_groups):
        y = jnp.dot(h, w_ref[:, n * GROUP:(n + 1) * GROUP], preferred_element_type=F32)
        if n in (COL_DQ, COL_DK):
            pieces = []
            for j in range(GROUP // LANES):
                yj = y[:, j * LANES:(j + 1) * LANES]
                partner = jnp.where(first_half,
                                    pltpu.roll(yj, LANES - DA_HEAD_DIM // 2, 1),
                                    pltpu.roll(yj, DA_HEAD_DIM // 2, 1))
                r = yj * cos + partner * sin
                if n == COL_DQ:
                    r = r * (DA_HEAD_DIM ** -0.5 * LOG2_E)
                pieces.append(r)
            y = jnp.concatenate(pieces, axis=1)
        o_ref[0, :, n * GROUP:(n + 1) * GROUP] = y.astype(BF16)


def _inproj_call(x, mod, nw, w_bf16, cos, sin):
    b, t, d = x.shape
    n = w_bf16.shape[1]
    tm = min(TM_PROJ, t)
    assert t % tm == 0
    return pl.pallas_call(
        _inproj_kernel,
        out_shape=jax.ShapeDtypeStruct((b, t, n), BF16),
        grid=(b, t // tm),
        in_specs=[
            pl.BlockSpec((1, tm, d), lambda i, j: (i, j, 0)),
            pl.BlockSpec((1, N_MOD, d), lambda i, j: (i, 0, 0)),
            pl.BlockSpec((1, d), lambda i, j: (0, 0)),
            pl.BlockSpec((d, n), lambda i, j: (0, 0), pipeline_mode=pl.Buffered(1)),
            pl.BlockSpec((tm, LANES), lambda i, j: (j, 0)),
            pl.BlockSpec((tm, LANES), lambda i, j: (j, 0)),
        ],
        out_specs=pl.BlockSpec((1, tm, n), lambda i, j: (i, j, 0)),
        compiler_params=pltpu.CompilerParams(
            dimension_semantics=("parallel", "parallel"), vmem_limit_bytes=VMEM_LIMIT_BYTES),
        name="norm_inproj",
    )(x, mod, nw.reshape(1, d), w_bf16, cos, sin)


def _hgrn_levels(ch):
    levels = []
    m = HG_SUB
    while m < ch:
        levels.append(m)
        m *= 2
    return levels


@functools.lru_cache(maxsize=None)
def _hgrn_constants(ch, reverse):
    idx = np.arange(ch)
    nsb = ch // HG_SUB
    order = (idx % nsb) * HG_SUB + idx // nsb
    if reverse:
        cum = (idx[None, :] >= idx[:, None]).astype(np.float32)
    else:
        cum = (idx[None, :] <= idx[:, None]).astype(np.float32)
    refs = []
    masks = []
    for m in _hgrn_levels(ch):
        pair = idx // (2 * m)
        in_second = (idx % (2 * m)) >= m
        ref = pair * 2 * m + (m if reverse else m - 1)
        refs.append(cum[ref[::HG_SUB]])
        later = ~in_second if reverse else in_second
        earlier = ~later
        masks.append(((pair[:, None] == pair[None, :]) & later[:, None] & earlier[None, :]).astype(np.float32))
    sel = np.ix_(order, order)
    w = np.concatenate([cum[sel]] + [rf[:, order] for rf in refs], axis=0)
    masks = np.stack([mk[sel] for mk in masks], axis=0)
    perm = np.eye(ch, dtype=np.float32)[order]
    return perm, w, masks


def _silu_exp2(x):
    return x / (1.0 + jnp.exp2(-LOG2_E * x))


def _hgrn_kernel(*refs, reverse, final, ch, n_chunks):
    if final:
        (hq_ref, hf_ref, hi_ref, hg_ref, of_ref, lb_ref, gw_ref, perm_ref, permt_ref, w_ref, mask_ref,
         o_ref, st_ref) = refs
    else:
        (hq_ref, hf_ref, hi_ref, lb_ref, perm_ref, w_ref, mask_ref, o_ref, st_ref) = refs
    n_lev = len(_hgrn_levels(ch))
    nsb = ch // HG_SUB

    @pl.when(pl.program_id(1) == 0)
    def _():
        st_ref[...] = jnp.zeros_like(st_ref)

    lb = lb_ref[...]
    lb_floor = jnp.maximum(lb, LOG_FLOOR)
    one_m_lb = 1.0 - lb
    ones = jnp.ones((LANES, LANES), BF16)

    def slab(x, t):
        return x[t * nsb:(t + 1) * nsb]

    heads = range(HG_HEADS)
    pairs = [(t, s) for t in range(HG_SUB) for s in range(HG_SUB) if ((s >= t) if reverse else (s <= t))]

    def chunk_body(i, carry):
        c = (n_chunks - 1 - i) if reverse else i
        rows = pl.ds(pl.multiple_of(c * ch, ch), ch)
        parts = [hq_ref[0, rows, :], hf_ref[0, rows, :], hi_ref[0, rows, :]]
        if final:
            parts.append(hg_ref[0, rows, :])
        xp = jnp.dot(perm_ref[...], jnp.concatenate(parts, axis=1), preferred_element_type=F32)

        def col(group, h):
            return xp[:, group * HG_WIDTH + h * HG_DK:group * HG_WIDTH + (h + 1) * HG_DK]

        q, kk, v, vb, g_split = [], [], [], [], []
        for h in heads:
            cols = slice(h * HG_DK, (h + 1) * HG_DK)
            z = col(1, h)
            e = jnp.exp2(-LOG2_E * jnp.abs(z))
            r = 1.0 / (1.0 + e)
            er = e * r
            pos = z >= 0.0
            g = jnp.log2(lb_floor[:, cols] + one_m_lb[:, cols] * jnp.where(pos, r, er))
            kk.append(one_m_lb[:, cols] * jnp.where(pos, er, r))
            q.append(_silu_exp2(col(0, h)))
            v.append(col(2, h))
            vb.append(col(2, h).astype(BF16))
            g_hi = g.astype(BF16)
            g_split += [g_hi, (g - g_hi.astype(F32)).astype(BF16)]

        y_all = jnp.dot(w_ref[...], jnp.concatenate(g_split, axis=1), preferred_element_type=F32)
        y = [y_all[:, 2 * h * HG_DK:(2 * h + 1) * HG_DK] + y_all[:, (2 * h + 1) * HG_DK:(2 * h + 2) * HG_DK]
             for h in heads]
        b = [y[h][0:ch] for h in heads]
        b_end = [b[h][0:1] if reverse else b[h][ch - 1:ch] for h in heads]

        a = []
        for h in heads:
            ah = None
            for l in range(n_lev):
                ref = y[h][ch + l * nsb:ch + (l + 1) * nsb]
                el = jnp.exp2(-jnp.abs(b[h] - jnp.concatenate([ref] * HG_SUB, axis=0)))
                al = lax.dot_general((q[h] * el).astype(BF16), (kk[h] * el).astype(BF16), NT_DIMS,
                                     preferred_element_type=F32) * mask_ref[l]
                ah = al if ah is None else ah + al
            a.append(ah.astype(BF16))

        pieces = []
        for h in heads:
            for t, s in pairs:
                p = slab(q[h], t) * slab(kk[h], s)
                if s != t:
                    p = p * jnp.exp2(slab(b[h], t) - slab(b[h], s))
                pieces.append(p)
        rsum = jnp.dot(jnp.concatenate(pieces, axis=0).astype(BF16), ones, preferred_element_type=F32)

        outs = []
        for h in heads:
            cols = slice(h * HG_DK, (h + 1) * HG_DK)
            od = [None] * HG_SUB
            for n, (t, s) in enumerate(pairs):
                n0 = (h * len(pairs) + n) * nsb
                term = rsum[n0:n0 + nsb] * slab(v[h], s)
                od[t] = term if od[t] is None else od[t] + term
            qs = (q[h] * jnp.exp2(b[h])).astype(BF16)
            ks = (kk[h] * jnp.exp2(b_end[h] - b[h])).astype(BF16)
            st = st_ref[h]
            o = (jnp.dot(a[h], vb[h], preferred_element_type=F32) + jnp.concatenate(od, axis=0)
                 + lax.dot_general(qs, st.astype(BF16), NT_DIMS, preferred_element_type=F32))
            st_ref[h] = (st * jnp.exp2(b_end[h])
                         + lax.dot_general(vb[h], ks, TN_DIMS, preferred_element_type=F32))
            if final:
                o = o + of_ref[0, rows, cols]
                o = _rms(o, gw_ref[...], NORM_EPS) * _silu_exp2(col(3, h))
                outs.append(o.astype(BF16))
            else:
                o_ref[0, rows, cols] = o
        if final:
            y_out = jnp.dot(permt_ref[...], jnp.concatenate(outs, axis=1), preferred_element_type=F32)
            o_ref[0, rows, :] = y_out.astype(o_ref.dtype)
        return carry

    lax.fori_loop(0, n_chunks, chunk_body, 0, unroll=2)


def _hgrn_call(proj, lb_dir, gnorm_w, o_fwd, *, reverse):
    b, t, _ = proj.shape
    final = o_fwd is not None
    ch = HG_CHUNK
    tile = min(HG_TILE, t)
    assert t % tile == 0 and tile % ch == 0
    nt = t // tile
    perm_np, w_np, mask_np = _hgrn_constants(ch, reverse)

    def tok(j):
        return (nt - 1 - j) if reverse else j

    def col_spec(group):
        return pl.BlockSpec((1, tile, GROUP), lambda i, j: (i, tok(j), group))

    def const_spec(shape):
        return pl.BlockSpec(shape, lambda i, j: (0,) * len(shape))

    in_specs = [col_spec(COL_HQ), col_spec(COL_HFB if reverse else COL_HFF), col_spec(COL_HI)]
    args = [proj, proj, proj]
    if final:
        in_specs += [col_spec(COL_HG), pl.BlockSpec((1, tile, HG_WIDTH), lambda i, j: (i, tok(j), 0))]
        args += [proj, o_fwd]
    in_specs.append(const_spec((1, HG_WIDTH)))
    args.append(lb_dir.reshape(1, HG_WIDTH))
    if final:
        in_specs.append(const_spec((1, HG_DK)))
        args.append(gnorm_w.reshape(1, HG_DK))
    in_specs.append(const_spec(perm_np.shape))
    args.append(jnp.asarray(perm_np, BF16))
    if final:
        in_specs.append(const_spec(perm_np.shape))
        args.append(jnp.asarray(perm_np.T, BF16))
    in_specs += [const_spec(w_np.shape), const_spec(mask_np.shape)]
    args += [jnp.asarray(w_np, BF16), jnp.asarray(mask_np, F32)]

    kern = functools.partial(_hgrn_kernel, reverse=reverse, final=final, ch=ch, n_chunks=tile // ch)
    return pl.pallas_call(
        kern,
        out_shape=jax.ShapeDtypeStruct((b, t, HG_WIDTH), BF16 if final else F32),
        grid=(b, nt),
        in_specs=in_specs,
        out_specs=pl.BlockSpec((1, tile, HG_WIDTH), lambda i, j: (i, tok(j), 0)),
        scratch_shapes=[pltpu.VMEM((HG_HEADS, HG_DK, HG_DK), F32)],
        compiler_params=pltpu.CompilerParams(
            dimension_semantics=("parallel", "arbitrary"), vmem_limit_bytes=VMEM_LIMIT_BYTES),
        name="hgrn2_bwd" if reverse else "hgrn2_fwd",
    )(*args)


def _attn_kernel(q_ref, k_ref, v_ref, lq1_ref, lk1_ref, lq2_ref, lk2_ref, sw_ref, o_ref,
                 *, tq, tkc, n_kv, lambda_init):
    q = q_ref[0]
    lane = lax.broadcasted_iota(jnp.int32, q.shape, 1)
    zero = jnp.zeros_like(q)
    qs = jnp.concatenate([jnp.where(lane < DA_HEAD_DIM, q, zero),
                          jnp.where(lane >= DA_HEAD_DIM, q, zero)], axis=0)

    def body(j, carry):
        m_prev, l_prev, acc = carry
        rows = pl.ds(pl.multiple_of(j * tkc, tkc), tkc)
        k = k_ref[0, rows, :]
        v = v_ref[0, rows, :]
        s = lax.dot_general(qs, k, NT_DIMS, preferred_element_type=F32)
        m_new = jnp.maximum(m_prev, jnp.max(s, axis=-1, keepdims=True))
        alpha = jnp.exp2(m_prev - m_new)
        p = jnp.exp2(s - m_new)
        l_new = alpha * l_prev + jnp.sum(p, axis=-1, keepdims=True)
        acc = alpha * acc + jnp.dot(p.astype(BF16), v, preferred_element_type=F32)
        return m_new, l_new, acc

    init = (jnp.full((2 * tq, 1), -jnp.inf, F32), jnp.zeros((2 * tq, 1), F32),
            jnp.zeros((2 * tq, 2 * DA_HEAD_DIM), F32))
    _, l_fin, acc = lax.fori_loop(0, n_kv, body, init, unroll=True)

    o = acc / l_fin
    lam = (jnp.exp(jnp.sum(lq1_ref[...] * lk1_ref[...], axis=-1, keepdims=True))
           - jnp.exp(jnp.sum(lq2_ref[...] * lk2_ref[...], axis=-1, keepdims=True)) + lambda_init)
    od = o[0:tq] - lam * o[tq:2 * tq]
    od = _rms(od, sw_ref[...], SUBLN_EPS) * (1.0 - lambda_init)
    o_ref[0] = od.astype(o_ref.dtype)


def _attn_call(proj, lq1, lk1, lq2, lk2, subln_w, lambda_init):
    b, t, _ = proj.shape
    tq = min(ATT_TQ, t)
    tkc = min(ATT_TK, t)
    assert t % tq == 0 and t % tkc == 0
    head_w = 2 * DA_HEAD_DIM
    blocks_per_group = GROUP // head_w

    def lam_spec():
        return pl.BlockSpec((1, DA_HEAD_DIM), lambda i, h, qi: (0, 0))

    kern = functools.partial(_attn_kernel, tq=tq, tkc=tkc, n_kv=t // tkc, lambda_init=lambda_init)
    return pl.pallas_call(
        kern,
        out_shape=jax.ShapeDtypeStruct((b, t, DA_WIDTH), BF16),
        grid=(b, DA_HEADS, t // tq),
        in_specs=[
            pl.BlockSpec((1, tq, head_w), lambda i, h, qi: (i, qi, COL_DQ * blocks_per_group + h)),
            pl.BlockSpec((1, t, head_w), lambda i, h, qi: (i, 0, COL_DK * blocks_per_group + h)),
            pl.BlockSpec((1, t, head_w), lambda i, h, qi: (i, 0, COL_DV * blocks_per_group + h)),
            lam_spec(), lam_spec(), lam_spec(), lam_spec(),
            pl.BlockSpec((1, head_w), lambda i, h, qi: (0, 0)),
        ],
        out_specs=pl.BlockSpec((1, tq, head_w), lambda i, h, qi: (i, qi, h)),
        compiler_params=pltpu.CompilerParams(
            dimension_semantics=("parallel", "parallel", "arbitrary"),
            vmem_limit_bytes=VMEM_LIMIT_BYTES),
        name="diff_attn",
    )(proj, proj, proj, lq1.reshape(1, -1), lk1.reshape(1, -1), lq2.reshape(1, -1), lk2.reshape(1, -1),
      subln_w.reshape(1, -1))


def _outffn_kernel(x_ref, ohg_ref, oda_ref, mod_ref, nw_ref, wo_ref, wg_ref, wu_ref, wd_ref, o_ref):
    x = x_ref[0]
    mix = (jnp.dot(ohg_ref[0], wo_ref[0:HG_WIDTH, :], preferred_element_type=F32)
           + jnp.dot(oda_ref[0], wo_ref[HG_WIDTH:HG_WIDTH + DA_WIDTH, :], preferred_element_type=F32))
    x1 = x + mod_ref[0, 2:3, :] * _rms(mix, nw_ref[0:1, :], NORM_EPS)
    h = _rms(x1, nw_ref[1:2, :], NORM_EPS) * (1.0 + mod_ref[0, 4:5, :]) + mod_ref[0, 3:4, :]
    h = h.astype(BF16)
    hidden = wg_ref.shape[1]
    f = None
    for c in range(hidden // FFN_CHUNK):
        cs = slice(c * FFN_CHUNK, (c + 1) * FFN_CHUNK)
        g = jnp.dot(h, wg_ref[:, cs], preferred_element_type=F32)
        u = jnp.dot(h, wu_ref[:, cs], preferred_element_type=F32)
        a = (_silu(g) * u).astype(BF16)
        fc = jnp.dot(a, wd_ref[cs, :], preferred_element_type=F32)
        f = fc if f is None else f + fc
    o_ref[0] = x1 + mod_ref[0, 5:6, :] * _rms(f, nw_ref[2:3, :], NORM_EPS)


def _outffn_call(x, o_hg, o_da, mod, norms, wo, wg, wu, wd):
    b, t, d = x.shape
    hidden = wg.shape[1]
    tm = min(TM_FFN, t)
    assert t % tm == 0 and hidden % FFN_CHUNK == 0

    def resident(shape):
        return pl.BlockSpec(shape, lambda i, j: (0, 0), pipeline_mode=pl.Buffered(1))

    return pl.pallas_call(
        _outffn_kernel,
        out_shape=jax.ShapeDtypeStruct((b, t, d), F32),
        grid=(b, t // tm),
        in_specs=[
            pl.BlockSpec((1, tm, d), lambda i, j: (i, j, 0)),
            pl.BlockSpec((1, tm, HG_WIDTH), lambda i, j: (i, j, 0)),
            pl.BlockSpec((1, tm, DA_WIDTH), lambda i, j: (i, j, 0)),
            pl.BlockSpec((1, N_MOD, d), lambda i, j: (i, 0, 0)),
            pl.BlockSpec((3, d), lambda i, j: (0, 0)),
            resident((HG_WIDTH + DA_WIDTH, d)),
            resident((d, hidden)),
            resident((d, hidden)),
            resident((hidden, d)),
        ],
        out_specs=pl.BlockSpec((1, tm, d), lambda i, j: (i, j, 0)),
        compiler_params=pltpu.CompilerParams(
            dimension_semantics=("parallel", "parallel"), vmem_limit_bytes=VMEM_LIMIT_BYTES),
        name="outproj_ffn",
    )(x, o_hg, o_da, mod, norms, wo, wg, wu, wd)


def _rotary_tables(t):
    dh = DA_HEAD_DIM
    inv = 1.0 / (ROPE_THETA ** (jnp.arange(0, dh, 2, dtype=F32) / dh))
    ang = jnp.arange(t, dtype=F32)[:, None] * inv[None, :]
    cos = jnp.cos(ang)
    sin = jnp.sin(ang)
    reps = LANES // dh
    cos_t = jnp.tile(jnp.concatenate([cos, cos], axis=-1), (1, reps))
    sin_t = jnp.tile(jnp.concatenate([-sin, sin], axis=-1), (1, reps))
    return cos_t, sin_t


def kernel(x_prompt, x_sample, c_prompt, c_sample, w_ada, b_ada, norm_pre_mix, norm_post_mix, norm_pre_ffn, norm_post_ffn, w_in, hg_lower_bounds, hg_gnorm, da_lambda_q1, da_lambda_k1, da_lambda_q2, da_lambda_k2, da_subln, w_out, w_ffn_gate, w_ffn_up, w_ffn_down):
    depth = w_in.shape[0]
    d = x_prompt.shape[-1]

    lb_soft = jax.nn.softmax(hg_lower_bounds.astype(F32), axis=1)
    lb_all = jnp.cumsum(lb_soft, axis=1) - lb_soft[:, :1]

    c_all = jnp.concatenate([c_prompt, c_sample], axis=0)
    mod_all = _ada_call(c_all, w_ada, b_ada)

    w_in_b = w_in.astype(BF16)
    w_out_b = w_out.astype(BF16)
    wg_b = w_ffn_gate.astype(BF16)
    wu_b = w_ffn_up.astype(BF16)
    wd_b = w_ffn_down.astype(BF16)

    def trunk(x, row0):
        b, t, _ = x.shape
        cos, sin = _rotary_tables(t)
        for l in range(depth):
            mod = mod_all[l, row0:row0 + b].reshape(b, N_MOD, d)
            lambda_init = 0.8 - 0.6 * math.exp(-0.3 * l)
            proj = _inproj_call(x, mod, norm_pre_mix[l], w_in_b[l], cos, sin)
            o_f = _hgrn_call(proj, lb_all[0, l], None, None, reverse=False)
            o_hg = _hgrn_call(proj, lb_all[1, l], hg_gnorm[l], o_f, reverse=True)
            o_da = _attn_call(proj, da_lambda_q1[l], da_lambda_k1[l], da_lambda_q2[l], da_lambda_k2[l],
                              da_subln[l], lambda_init)
            norms = jnp.stack([norm_post_mix[l], norm_pre_ffn[l], norm_post_ffn[l]], axis=0)
            x = _outffn_call(x, o_hg, o_da, mod, norms, w_out_b[l], wg_b[l], wu_b[l], wd_b[l])
        return x

    y_prompt = trunk(x_prompt, 0)
    y_sample = trunk(x_sample, c_prompt.shape[0])
    return (y_prompt, y_sample)
```

```python
import functools
import math

import numpy as np
import jax
import jax.numpy as jnp
from jax import lax
from jax.experimental import pallas as pl
from jax.experimental.pallas import tpu as pltpu

F32 = jnp.float32
BF16 = jnp.bfloat16

HG_WIDTH = 512
HG_HEADS = 4
HG_DK = 128
DA_WIDTH = 512
DA_HEADS = 4
DA_HEAD_DIM = 64
ROPE_THETA = 10000.0
NORM_EPS = 1e-6
SUBLN_EPS = 1e-5
LOG_FLOOR = 1e-30
LOG2_E = math.log2(math.e)
N_MOD = 6

LANES = 128
SUBLANES = 8
VMEM_LIMIT_BYTES = 56 * 1024 * 1024

COL_HQ, COL_HFF, COL_HFB, COL_HI, COL_HG, COL_DQ, COL_DK, COL_DV = range(8)
GROUP = 512

TM_PROJ = 512
TM_FFN = 512
FFN_CHUNK = 256
HG_CHUNK = 128
HG_TILE = 512
HG_SUB = SUBLANES
ATT_TQ = 512
ATT_TK = 1024

NT_DIMS = (((1,), (1,)), ((), ()))
TN_DIMS = (((0,), (0,)), ((), ()))


def _silu(x):
    return x * jax.nn.sigmoid(x)


def _rms(x, w, eps):
    ms = jnp.mean(x * x, axis=-1, keepdims=True)
    return x * lax.rsqrt(ms + eps) * w


def _ada_kernel(c_ref, w_ref, b_ref, o_ref):
    s = _silu(c_ref[...]).astype(BF16)
    w = w_ref[0].astype(BF16)
    o_ref[0] = jnp.dot(s, w, preferred_element_type=F32) + b_ref[0]


def _ada_call(c_all, w_ada, b_ada):
    depth, d, n = w_ada.shape
    rows = c_all.shape[0]
    tn = 1536
    assert n % tn == 0
    return pl.pallas_call(
        _ada_kernel,
        out_shape=jax.ShapeDtypeStruct((depth, rows, n), F32),
        grid=(depth, n // tn),
        in_specs=[
            pl.BlockSpec((rows, d), lambda l, j: (0, 0)),
            pl.BlockSpec((1, d, tn), lambda l, j: (l, 0, j)),
            pl.BlockSpec((1, 1, tn), lambda l, j: (l, 0, j)),
        ],
        out_specs=pl.BlockSpec((1, rows, tn), lambda l, j: (l, 0, j)),
        compiler_params=pltpu.CompilerParams(
            dimension_semantics=("parallel", "parallel"), vmem_limit_bytes=VMEM_LIMIT_BYTES),
        name="adaln_mod",
    )(c_all, w_ada, b_ada.reshape(depth, 1, n))


def _inproj_kernel(x_ref, mod_ref, nw_ref, w_ref, cos_ref, sin_ref, o_ref):
    x = x_ref[0]
    h = _rms(x, nw_ref[...], NORM_EPS) * (1.0 + mod_ref[0, 1:2, :]) + mod_ref[0, 0:1, :]
    h = h.astype(BF16)
    cos = cos_ref[...]
    sin = sin_ref[...]
    lane = lax.broadcasted_iota(jnp.int32, cos.shape, 1)
    first_half = (lane % DA_HEAD_DIM) < (DA_HEAD_DIM // 2)
    n_groups = w_ref.shape[1] // GROUP
    for n in range(n_groups):
        y = jnp.dot(h, w_ref[:, n * GROUP:(n + 1) * GROUP], preferred_element_type=F32)
        if n in (COL_DQ, COL_DK):
            pieces = []
            for j in range(GROUP // LANES):
                yj = y[:, j * LANES:(j + 1) * LANES]
                partner = jnp.where(first_half,
                                    pltpu.roll(yj, LANES - DA_HEAD_DIM // 2, 1),
                                    pltpu.roll(yj, DA_HEAD_DIM // 2, 1))
                r = yj * cos + partner * sin
                if n == COL_DQ:
                    r = r * (DA_HEAD_DIM ** -0.5 * LOG2_E)
                pieces.append(r)
            y = jnp.concatenate(pieces, axis=1)
        o_ref[0, :, n * GROUP:(n + 1) * GROUP] = y.astype(BF16)


def _inproj_call(x, mod, nw, w_bf16, cos, sin):
    b, t, d = x.shape
    n = w_bf16.shape[1]
    tm = min(TM_PROJ, t)
    assert t % tm == 0
    return pl.pallas_call(
        _inproj_kernel,
        out_shape=jax.ShapeDtypeStruct((b, t, n), BF16),
        grid=(b, t // tm),
        in_specs=[
            pl.BlockSpec((1, tm, d), lambda i, j: (i, j, 0)),
            pl.BlockSpec((1, N_MOD, d), lambda i, j: (i, 0, 0)),
            pl.BlockSpec((1, d), lambda i, j: (0, 0)),
            pl.BlockSpec((d, n), lambda i, j: (0, 0), pipeline_mode=pl.Buffered(1)),
            pl.BlockSpec((tm, LANES), lambda i, j: (j, 0)),
            pl.BlockSpec((tm, LANES), lambda i, j: (j, 0)),
        ],
        out_specs=pl.BlockSpec((1, tm, n), lambda i, j: (i, j, 0)),
        compiler_params=pltpu.CompilerParams(
            dimension_semantics=("parallel", "parallel"), vmem_limit_bytes=VMEM_LIMIT_BYTES),
        name="norm_inproj",
    )(x, mod, nw.reshape(1, d), w_bf16, cos, sin)


def _hgrn_levels(ch):
    levels = []
    m = HG_SUB
    while m < ch:
        levels.append(m)
        m *= 2
    return levels


@functools.lru_cache(maxsize=None)
def _hgrn_constants(ch, reverse):
    idx = np.arange(ch)
    nsb = ch // HG_SUB
    order = (idx % nsb) * HG_SUB + idx // nsb
    if reverse:
        cum = (idx[None, :] >= idx[:, None]).astype(np.float32)
    else:
        cum = (idx[None, :] <= idx[:, None]).astype(np.float32)
    refs = []
    masks = []
    for m in _hgrn_levels(ch):
        pair = idx // (2 * m)
        in_second = (idx % (2 * m)) >= m
        ref = pair * 2 * m + (m if reverse else m - 1)
        refs.append(cum[ref[::HG_SUB]])
        later = ~in_second if reverse else in_second
        earlier = ~later
        masks.append(((pair[:, None] == pair[None, :]) & later[:, None] & earlier[None, :]).astype(np.float32))
    sel = np.ix_(order, order)
    w = np.concatenate([cum[sel]] + [rf[:, order] for rf in refs], axis=0)
    masks = np.stack([mk[sel] for mk in masks], axis=0)
    perm = np.eye(ch, dtype=np.float32)[order]
    return perm, w, masks


def _silu_exp2(x):
    return x / (1.0 + jnp.exp2(-LOG2_E * x))


def _hgrn_kernel(*refs, reverse, final, ch, n_chunks):
    if final:
        (hq_ref, hf_ref, hi_ref, hg_ref, of_ref, lb_ref, gw_ref, perm_ref, permt_ref, w_ref, mask_ref,
         o_ref, st_ref) = refs
    else:
        (hq_ref, hf_ref, hi_ref, lb_ref, perm_ref, w_ref, mask_ref, o_ref, st_ref) = refs
    n_lev = len(_hgrn_levels(ch))
    nsb = ch // HG_SUB

    @pl.when(pl.program_id(1) == 0)
    def _():
        st_ref[...] = jnp.zeros_like(st_ref)

    lb = lb_ref[...]
    lb_floor = jnp.maximum(lb, LOG_FLOOR)
    one_m_lb = 1.0 - lb
    ones = jnp.ones((LANES, LANES), BF16)

    def slab(x, t):
        return x[t * nsb:(t + 1) * nsb]

    heads = range(HG_HEADS)
    pairs = [(t, s) for t in range(HG_SUB) for s in range(HG_SUB) if ((s >= t) if reverse else (s <= t))]

    def chunk_body(i, carry):
        c = (n_chunks - 1 - i) if reverse else i
        rows = pl.ds(pl.multiple_of(c * ch, ch), ch)
        parts = [hq_ref[0, rows, :], hf_ref[0, rows, :], hi_ref[0, rows, :]]
        if final:
            parts.append(hg_ref[0, rows, :])
        xp = jnp.dot(perm_ref[...], jnp.concatenate(parts, axis=1), preferred_element_type=F32)

        def col(group, h):
            return xp[:, group * HG_WIDTH + h * HG_DK:group * HG_WIDTH + (h + 1) * HG_DK]

        q, kk, v, vb, g_split = [], [], [], [], []
        for h in heads:
            cols = slice(h * HG_DK, (h + 1) * HG_DK)
            z = col(1, h)
            e = jnp.exp2(-LOG2_E * jnp.abs(z))
            r = 1.0 / (1.0 + e)
            er = e * r
            pos = z >= 0.0
            g = jnp.log2(lb_floor[:, cols] + one_m_lb[:, cols] * jnp.where(pos, r, er))
            kk.append(one_m_lb[:, cols] * jnp.where(pos, er, r))
            q.append(_silu_exp2(col(0, h)))
            v.append(col(2, h))
            vb.append(col(2, h).astype(BF16))
            g_hi = g.astype(BF16)
            g_split += [g_hi, (g - g_hi.astype(F32)).astype(BF16)]

        y_all = jnp.dot(w_ref[...], jnp.concatenate(g_split, axis=1), preferred_element_type=F32)
        y = [y_all[:, 2 * h * HG_DK:(2 * h + 1) * HG_DK] + y_all[:, (2 * h + 1) * HG_DK:(2 * h + 2) * HG_DK]
             for h in heads]
        b = [y[h][0:ch] for h in heads]
        b_end = [b[h][0:1] if reverse else b[h][ch - 1:ch] for h in heads]

        a = []
        for h in heads:
            ah = None
            for l in range(n_lev):
                ref = y[h][ch + l * nsb:ch + (l + 1) * nsb]
                el = jnp.exp2(-jnp.abs(b[h] - jnp.concatenate([ref] * HG_SUB, axis=0)))
                al = lax.dot_general((q[h] * el).astype(BF16), (kk[h] * el).astype(BF16), NT_DIMS,
                                     preferred_element_type=F32) * mask_ref[l]
                ah = al if ah is None else ah + al
            a.append(ah.astype(BF16))

        pieces = []
        for h in heads:
            for t, s in pairs:
                p = slab(q[h], t) * slab(kk[h], s)
                if s != t:
                    p = p * jnp.exp2(slab(b[h], t) - slab(b[h], s))
                pieces.append(p)
        rsum = jnp.dot(jnp.concatenate(pieces, axis=0).astype(BF16), ones, preferred_element_type=F32)

        outs = []
        for h in heads:
            cols = slice(h * HG_DK, (h + 1) * HG_DK)
            od = [None] * HG_SUB
            for n, (t, s) in enumerate(pairs):
                n0 = (h * len(pairs) + n) * nsb
                term = rsum[n0:n0 + nsb] * slab(v[h], s)
                od[t] = term if od[t] is None else od[t] + term
            qs = (q[h] * jnp.exp2(b[h])).astype(BF16)
            ks = (kk[h] * jnp.exp2(b_end[h] - b[h])).astype(BF16)
            st = st_ref[h]
            o = (jnp.dot(a[h], vb[h], preferred_element_type=F32) + jnp.concatenate(od, axis=0)
                 + lax.dot_general(qs, st.astype(BF16), NT_DIMS, preferred_element_type=F32))
            st_ref[h] = (st * jnp.exp2(b_end[h])
                         + lax.dot_general(vb[h], ks, TN_DIMS, preferred_element_type=F32))
            if final:
                o = o + of_ref[0, rows, cols]
                o = _rms(o, gw_ref[...], NORM_EPS) * _silu_exp2(col(3, h))
                outs.append(o.astype(BF16))
            else:
                o_ref[0, rows, cols] = o
        if final:
            y_out = jnp.dot(permt_ref[...], jnp.concatenate(outs, axis=1), preferred_element_type=F32)
            o_ref[0, rows, :] = y_out.astype(o_ref.dtype)
        return carry

    lax.fori_loop(0, n_chunks, chunk_body, 0, unroll=2)


def _hgrn_call(proj, lb_dir, gnorm_w, o_fwd, *, reverse):
    b, t, _ = proj.shape
    final = o_fwd is not None
    ch = HG_CHUNK
    tile = min(HG_TILE, t)
    assert t % tile == 0 and tile % ch == 0
    nt = t // tile
    perm_np, w_np, mask_np = _hgrn_constants(ch, reverse)

    def tok(j):
        return (nt - 1 - j) if reverse else j

    def col_spec(group):
        return pl.BlockSpec((1, tile, GROUP), lambda i, j: (i, tok(j), group))

    def const_spec(shape):
        return pl.BlockSpec(shape, lambda i, j: (0,) * len(shape))

    in_specs = [col_spec(COL_HQ), col_spec(COL_HFB if reverse else COL_HFF), col_spec(COL_HI)]
    args = [proj, proj, proj]
    if final:
        in_specs += [col_spec(COL_HG), pl.BlockSpec((1, tile, HG_WIDTH), lambda i, j: (i, tok(j), 0))]
        args += [proj, o_fwd]
    in_specs.append(const_spec((1, HG_WIDTH)))
    args.append(lb_dir.reshape(1, HG_WIDTH))
    if final:
        in_specs.append(const_spec((1, HG_DK)))
        args.append(gnorm_w.reshape(1, HG_DK))
    in_specs.append(const_spec(perm_np.shape))
    args.append(jnp.asarray(perm_np, BF16))
    if final:
        in_specs.append(const_spec(perm_np.shape))
        args.append(jnp.asarray(perm_np.T, BF16))
    in_specs += [const_spec(w_np.shape), const_spec(mask_np.shape)]
    args += [jnp.asarray(w_np, BF16), jnp.asarray(mask_np, F32)]

    kern = functools.partial(_hgrn_kernel, reverse=reverse, final=final, ch=ch, n_chunks=tile // ch)
    return pl.pallas_call(
        kern,
        out_shape=jax.ShapeDtypeStruct((b, t, HG_WIDTH), BF16 if final else F32),
        grid=(b, nt),
        in_specs=in_specs,
        out_specs=pl.BlockSpec((1, tile, HG_WIDTH), lambda i, j: (i, tok(j), 0)),
        scratch_shapes=[pltpu.VMEM((HG_HEADS, HG_DK, HG_DK), F32)],
        compiler_params=pltpu.CompilerParams(
            dimension_semantics=("parallel", "arbitrary"), vmem_limit_bytes=VMEM_LIMIT_BYTES),
        name="hgrn2_bwd" if reverse else "hgrn2_fwd",
    )(*args)


def _attn_kernel(q_ref, k_ref, v_ref, lq1_ref, lk1_ref, lq2_ref, lk2_ref, sw_ref, o_ref,
                 *, tq, tkc, n_kv, lambda_init):
    q = q_ref[0]
    lane = lax.broadcasted_iota(jnp.int32, q.shape, 1)
    zero = jnp.zeros_like(q)
    qs = jnp.concatenate([jnp.where(lane < DA_HEAD_DIM, q, zero),
                          jnp.where(lane >= DA_HEAD_DIM, q, zero)], axis=0)

    def body(j, carry):
        m_prev, l_prev, acc = carry
        rows = pl.ds(pl.multiple_of(j * tkc, tkc), tkc)
        k = k_ref[0, rows, :]
        v = v_ref[0, rows, :]
        s = lax.dot_general(qs, k, NT_DIMS, preferred_element_type=F32)
        m_new = jnp.maximum(m_prev, jnp.max(s, axis=-1, keepdims=True))
        alpha = jnp.exp2(m_prev - m_new)
        p = jnp.exp2(s - m_new)
        l_new = alpha * l_prev + jnp.sum(p, axis=-1, keepdims=True)
        acc = alpha * acc + jnp.dot(p.astype(BF16), v, preferred_element_type=F32)
        return m_new, l_new, acc

    init = (jnp.full((2 * tq, 1), -jnp.inf, F32), jnp.zeros((2 * tq, 1), F32),
            jnp.zeros((2 * tq, 2 * DA_HEAD_DIM), F32))
    _, l_fin, acc = lax.fori_loop(0, n_kv, body, init, unroll=True)

    o = acc / l_fin
    lam = (jnp.exp(jnp.sum(lq1_ref[...] * lk1_ref[...], axis=-1, keepdims=True))
           - jnp.exp(jnp.sum(lq2_ref[...] * lk2_ref[...], axis=-1, keepdims=True)) + lambda_init)
    od = o[0:tq] - lam * o[tq:2 * tq]
    od = _rms(od, sw_ref[...], SUBLN_EPS) * (1.0 - lambda_init)
    o_ref[0] = od.astype(o_ref.dtype)


def _attn_call(proj, lq1, lk1, lq2, lk2, subln_w, lambda_init):
    b, t, _ = proj.shape
    tq = min(ATT_TQ, t)
    tkc = min(ATT_TK, t)
    assert t % tq == 0 and t % tkc == 0
    head_w = 2 * DA_HEAD_DIM
    blocks_per_group = GROUP // head_w

    def lam_spec():
        return pl.BlockSpec((1, DA_HEAD_DIM), lambda i, h, qi: (0, 0))

    kern = functools.partial(_attn_kernel, tq=tq, tkc=tkc, n_kv=t // tkc, lambda_init=lambda_init)
    return pl.pallas_call(
        kern,
        out_shape=jax.ShapeDtypeStruct((b, t, DA_WIDTH), BF16),
        grid=(b, DA_HEADS, t // tq),
        in_specs=[
            pl.BlockSpec((1, tq, head_w), lambda i, h, qi: (i, qi, COL_DQ * blocks_per_group + h)),
            pl.BlockSpec((1, t, head_w), lambda i, h, qi: (i, 0, COL_DK * blocks_per_group + h)),
            pl.BlockSpec((1, t, head_w), lambda i, h, qi: (i, 0, COL_DV * blocks_per_group + h)),
            lam_spec(), lam_spec(), lam_spec(), lam_spec(),
            pl.BlockSpec((1, head_w), lambda i, h, qi: (0, 0)),
        ],
        out_specs=pl.BlockSpec((1, tq, head_w), lambda i, h, qi: (i, qi, h)),
        compiler_params=pltpu.CompilerParams(
            dimension_semantics=("parallel", "parallel", "arbitrary"),
            vmem_limit_bytes=VMEM_LIMIT_BYTES),
        name="diff_attn",
    )(proj, proj, proj, lq1.reshape(1, -1), lk1.reshape(1, -1), lq2.reshape(1, -1), lk2.reshape(1, -1),
      subln_w.reshape(1, -1))


def _outffn_kernel(x_ref, ohg_ref, oda_ref, mod_ref, nw_ref, wo_ref, wg_ref, wu_ref, wd_ref, o_ref):
    x = x_ref[0]
    mix = (jnp.dot(ohg_ref[0], wo_ref[0:HG_WIDTH, :], preferred_element_type=F32)
           + jnp.dot(oda_ref[0], wo_ref[HG_WIDTH:HG_WIDTH + DA_WIDTH, :], preferred_element_type=F32))
    x1 = x + mod_ref[0, 2:3, :] * _rms(mix, nw_ref[0:1, :], NORM_EPS)
    h = _rms(x1, nw_ref[1:2, :], NORM_EPS) * (1.0 + mod_ref[0, 4:5, :]) + mod_ref[0, 3:4, :]
    h = h.astype(BF16)
    hidden = wg_ref.shape[1]
    f = None
    for c in range(hidden // FFN_CHUNK):
        cs = slice(c * FFN_CHUNK, (c + 1) * FFN_CHUNK)
        g = jnp.dot(h, wg_ref[:, cs], preferred_element_type=F32)
        u = jnp.dot(h, wu_ref[:, cs], preferred_element_type=F32)
        a = (_silu_exp2(g) * u).astype(BF16)
        fc = jnp.dot(a, wd_ref[cs, :], preferred_element_type=F32)
        f = fc if f is None else f + fc
    o_ref[0] = x1 + mod_ref[0, 5:6, :] * _rms(f, nw_ref[2:3, :], NORM_EPS)


def _outffn_call(x, o_hg, o_da, mod, norms, wo, wg, wu, wd):
    b, t, d = x.shape
    hidden = wg.shape[1]
    tm = min(TM_FFN, t)
    assert t % tm == 0 and hidden % FFN_CHUNK == 0

    def resident(shape):
        return pl.BlockSpec(shape, lambda i, j: (0, 0), pipeline_mode=pl.Buffered(1))

    return pl.pallas_call(
        _outffn_kernel,
        out_shape=jax.ShapeDtypeStruct((b, t, d), F32),
        grid=(b, t // tm),
        in_specs=[
            pl.BlockSpec((1, tm, d), lambda i, j: (i, j, 0)),
            pl.BlockSpec((1, tm, HG_WIDTH), lambda i, j: (i, j, 0)),
            pl.BlockSpec((1, tm, DA_WIDTH), lambda i, j: (i, j, 0)),
            pl.BlockSpec((1, N_MOD, d), lambda i, j: (i, 0, 0)),
            pl.BlockSpec((3, d), lambda i, j: (0, 0)),
            resident((HG_WIDTH + DA_WIDTH, d)),
            resident((d, hidden)),
            resident((d, hidden)),
            resident((hidden, d)),
        ],
        out_specs=pl.BlockSpec((1, tm, d), lambda i, j: (i, j, 0)),
        compiler_params=pltpu.CompilerParams(
            dimension_semantics=("parallel", "parallel"), vmem_limit_bytes=VMEM_LIMIT_BYTES),
        name="outproj_ffn",
    )(x, o_hg, o_da, mod, norms, wo, wg, wu, wd)


def _rotary_tables(t):
    dh = DA_HEAD_DIM
    inv = 1.0 / (ROPE_THETA ** (jnp.arange(0, dh, 2, dtype=F32) / dh))
    ang = jnp.arange(t, dtype=F32)[:, None] * inv[None, :]
    cos = jnp.cos(ang)
    sin = jnp.sin(ang)
    reps = LANES // dh
    cos_t = jnp.tile(jnp.concatenate([cos, cos], axis=-1), (1, reps))
    sin_t = jnp.tile(jnp.concatenate([-sin, sin], axis=-1), (1, reps))
    return cos_t, sin_t


def kernel(x_prompt, x_sample, c_prompt, c_sample, w_ada, b_ada, norm_pre_mix, norm_post_mix, norm_pre_ffn, norm_post_ffn, w_in, hg_lower_bounds, hg_gnorm, da_lambda_q1, da_lambda_k1, da_lambda_q2, da_lambda_k2, da_subln, w_out, w_ffn_gate, w_ffn_up, w_ffn_down):
    depth = w_in.shape[0]
    d = x_prompt.shape[-1]

    lb_soft = jax.nn.softmax(hg_lower_bounds.astype(F32), axis=1)
    lb_all = jnp.cumsum(lb_soft, axis=1) - lb_soft[:, :1]

    c_all = jnp.concatenate([c_prompt, c_sample], axis=0)
    mod_all = _ada_call(c_all, w_ada, b_ada)

    w_in_b = w_in.astype(BF16)
    w_out_b = w_out.astype(BF16)
    wg_b = w_ffn_gate.astype(BF16)
    wu_b = w_ffn_up.astype(BF16)
    wd_b = w_ffn_down.astype(BF16)

    def trunk(x, row0):
        b, t, _ = x.shape
        cos, sin = _rotary_tables(t)
        for l in range(depth):
            mod = mod_all[l, row0:row0 + b].reshape(b, N_MOD, d)
            lambda_init = 0.8 - 0.6 * math.exp(-0.3 * l)
            proj = _inproj_call(x, mod, norm_pre_mix[l], w_in_b[l], cos, sin)
            o_f = _hgrn_call(proj, lb_all[0, l], None, None, reverse=False)
            o_hg = _hgrn_call(proj, lb_all[1, l], hg_gnorm[l], o_f, reverse=True)
            o_da = _attn_call(proj, da_lambda_q1[l], da_lambda_k1[l], da_lambda_q2[l], da_lambda_k2[l],
                              da_subln[l], lambda_init)
            norms = jnp.stack([norm_post_mix[l], norm_pre_ffn[l], norm_post_ffn[l]], axis=0)
            x = _outffn_call(x, o_hg, o_da, mod, norms, w_out_b[l], wg_b[l], wu_b[l], wd_b[l])
        return x

    y_prompt = trunk(x_prompt, 0)
    y_sample = trunk(x_sample, c_prompt.shape[0])
    return (y_prompt, y_sample)
```

```python
import functools
import math

import numpy as np
import jax
import jax.numpy as jnp
from jax import lax
from jax.experimental import pallas as pl
from jax.experimental.pallas import tpu as pltpu

F32 = jnp.float32
BF16 = jnp.bfloat16

HG_WIDTH = 512
HG_HEADS = 4
HG_DK = 128
DA_WIDTH = 512
DA_HEADS = 4
DA_HEAD_DIM = 64
ROPE_THETA = 10000.0
NORM_EPS = 1e-6
SUBLN_EPS = 1e-5
LOG_FLOOR = 1e-30
LOG2_E = math.log2(math.e)
N_MOD = 6

LANES = 128
SUBLANES = 8
VMEM_LIMIT_BYTES = 56 * 1024 * 1024

COL_HQ, COL_HFF, COL_HFB, COL_HI, COL_HG, COL_DQ, COL_DK, COL_DV = range(8)
GROUP = 512

TM_PROJ = 512
TM_FFN = 512
FFN_CHUNK = 256
HG_CHUNK = 128
HG_TILE = 512
HG_SUB = SUBLANES
ATT_TQ = 512
ATT_TK = 2048
ATT_SUB = 2

NT_DIMS = (((1,), (1,)), ((), ()))
TN_DIMS = (((0,), (0,)), ((), ()))


def _silu(x):
    return x * jax.nn.sigmoid(x)


def _rms(x, w, eps):
    ms = jnp.mean(x * x, axis=-1, keepdims=True)
    return x * lax.rsqrt(ms + eps) * w


def _ada_kernel(c_ref, w_ref, b_ref, o_ref):
    s = _silu(c_ref[...]).astype(BF16)
    w = w_ref[0].astype(BF16)
    o_ref[0] = jnp.dot(s, w, preferred_element_type=F32) + b_ref[0]


def _ada_call(c_all, w_ada, b_ada):
    depth, d, n = w_ada.shape
    rows = c_all.shape[0]
    tn = 1536
    assert n % tn == 0
    return pl.pallas_call(
        _ada_kernel,
        out_shape=jax.ShapeDtypeStruct((depth, rows, n), F32),
        grid=(depth, n // tn),
        in_specs=[
            pl.BlockSpec((rows, d), lambda l, j: (0, 0)),
            pl.BlockSpec((1, d, tn), lambda l, j: (l, 0, j)),
            pl.BlockSpec((1, 1, tn), lambda l, j: (l, 0, j)),
        ],
        out_specs=pl.BlockSpec((1, rows, tn), lambda l, j: (l, 0, j)),
        compiler_params=pltpu.CompilerParams(
            dimension_semantics=("parallel", "parallel"), vmem_limit_bytes=VMEM_LIMIT_BYTES),
        name="adaln_mod",
    )(c_all, w_ada, b_ada.reshape(depth, 1, n))


def _inproj_kernel(x_ref, mod_ref, nw_ref, w_ref, cos_ref, sin_ref, o_ref):
    x = x_ref[0]
    h = _rms(x, nw_ref[...], NORM_EPS) * (1.0 + mod_ref[0, 1:2, :]) + mod_ref[0, 0:1, :]
    h = h.astype(BF16)
    cos = cos_ref[...]
    sin = sin_ref[...]
    lane = lax.broadcasted_iota(jnp.int32, cos.shape, 1)
    first_half = (lane % DA_HEAD_DIM) < (DA_HEAD_DIM // 2)
    n_groups = w_ref.shape[1] // GROUP
    for n in range(n_groups):
        y = jnp.dot(h, w_ref[:, n * GROUP:(n + 1) * GROUP], preferred_element_type=F32)
        if n in (COL_DQ, COL_DK):
            pieces = []
            for j in range(GROUP // LANES):
                yj = y[:, j * LANES:(j + 1) * LANES]
                partner = jnp.where(first_half,
                                    pltpu.roll(yj, LANES - DA_HEAD_DIM // 2, 1),
                                    pltpu.roll(yj, DA_HEAD_DIM // 2, 1))
                r = yj * cos + partner * sin
                if n == COL_DQ:
                    r = r * (DA_HEAD_DIM ** -0.5 * LOG2_E)
                pieces.append(r)
            y = jnp.concatenate(pieces, axis=1)
        o_ref[0, :, n * GROUP:(n + 1) * GROUP] = y.astype(BF16)


def _inproj_call(x, mod, nw, w_bf16, cos, sin):
    b, t, d = x.shape
    n = w_bf16.shape[1]
    tm = min(TM_PROJ, t)
    assert t % tm == 0
    return pl.pallas_call(
        _inproj_kernel,
        out_shape=jax.ShapeDtypeStruct((b, t, n), BF16),
        grid=(b, t // tm),
        in_specs=[
            pl.BlockSpec((1, tm, d), lambda i, j: (i, j, 0)),
            pl.BlockSpec((1, N_MOD, d), lambda i, j: (i, 0, 0)),
            pl.BlockSpec((1, d), lambda i, j: (0, 0)),
            pl.BlockSpec((d, n), lambda i, j: (0, 0), pipeline_mode=pl.Buffered(1)),
            pl.BlockSpec((tm, LANES), lambda i, j: (j, 0)),
            pl.BlockSpec((tm, LANES), lambda i, j: (j, 0)),
        ],
        out_specs=pl.BlockSpec((1, tm, n), lambda i, j: (i, j, 0)),
        compiler_params=pltpu.CompilerParams(
            dimension_semantics=("parallel", "parallel"), vmem_limit_bytes=VMEM_LIMIT_BYTES),
        name="norm_inproj",
    )(x, mod, nw.reshape(1, d), w_bf16, cos, sin)


def _hgrn_levels(ch):
    levels = []
    m = HG_SUB
    while m < ch:
        levels.append(m)
        m *= 2
    return levels


@functools.lru_cache(maxsize=None)
def _hgrn_constants(ch, reverse):
    idx = np.arange(ch)
    nsb = ch // HG_SUB
    order = (idx % nsb) * HG_SUB + idx // nsb
    if reverse:
        cum = (idx[None, :] >= idx[:, None]).astype(np.float32)
    else:
        cum = (idx[None, :] <= idx[:, None]).astype(np.float32)
    refs = []
    masks = []
    for m in _hgrn_levels(ch):
        pair = idx // (2 * m)
        in_second = (idx % (2 * m)) >= m
        ref = pair * 2 * m + (m if reverse else m - 1)
        refs.append(cum[ref[::HG_SUB]])
        later = ~in_second if reverse else in_second
        earlier = ~later
        masks.append(((pair[:, None] == pair[None, :]) & later[:, None] & earlier[None, :]).astype(np.float32))
    sel = np.ix_(order, order)
    w = np.concatenate([cum[sel]] + [rf[:, order] for rf in refs], axis=0)
    masks = np.stack([mk[sel] for mk in masks], axis=0)
    perm = np.eye(ch, dtype=np.float32)[order]
    return perm, w, masks


def _silu_exp2(x):
    return x / (1.0 + jnp.exp2(-LOG2_E * x))


def _hgrn_kernel(*refs, reverse, final, ch, n_chunks):
    if final:
        (hq_ref, hf_ref, hi_ref, hg_ref, of_ref, lb_ref, gw_ref, perm_ref, permt_ref, w_ref, mask_ref,
         o_ref, st_ref) = refs
    else:
        (hq_ref, hf_ref, hi_ref, lb_ref, perm_ref, w_ref, mask_ref, o_ref, st_ref) = refs
    n_lev = len(_hgrn_levels(ch))
    nsb = ch // HG_SUB

    @pl.when(pl.program_id(1) == 0)
    def _():
        st_ref[...] = jnp.zeros_like(st_ref)

    lb = lb_ref[...]
    lb_floor = jnp.maximum(lb, LOG_FLOOR)
    one_m_lb = 1.0 - lb
    ones = jnp.ones((LANES, LANES), BF16)

    def slab(x, t):
        return x[t * nsb:(t + 1) * nsb]

    heads = range(HG_HEADS)
    pairs = [(t, s) for t in range(HG_SUB) for s in range(HG_SUB) if ((s >= t) if reverse else (s <= t))]

    def chunk_body(i, carry):
        c = (n_chunks - 1 - i) if reverse else i
        rows = pl.ds(pl.multiple_of(c * ch, ch), ch)
        parts = [hq_ref[0, rows, :], hf_ref[0, rows, :], hi_ref[0, rows, :]]
        if final:
            parts.append(hg_ref[0, rows, :])
        xp = jnp.dot(perm_ref[...], jnp.concatenate(parts, axis=1), preferred_element_type=F32)

        def col(group, h):
            return xp[:, group * HG_WIDTH + h * HG_DK:group * HG_WIDTH + (h + 1) * HG_DK]

        q, kk, v, vb, g_split = [], [], [], [], []
        for h in heads:
            cols = slice(h * HG_DK, (h + 1) * HG_DK)
            z = col(1, h)
            e = jnp.exp2(-LOG2_E * jnp.abs(z))
            r = 1.0 / (1.0 + e)
            er = e * r
            pos = z >= 0.0
            g = jnp.log2(lb_floor[:, cols] + one_m_lb[:, cols] * jnp.where(pos, r, er))
            kk.append(one_m_lb[:, cols] * jnp.where(pos, er, r))
            q.append(_silu_exp2(col(0, h)))
            v.append(col(2, h))
            vb.append(col(2, h).astype(BF16))
            g_hi = g.astype(BF16)
            g_split += [g_hi, (g - g_hi.astype(F32)).astype(BF16)]

        y_all = jnp.dot(w_ref[...], jnp.concatenate(g_split, axis=1), preferred_element_type=F32)
        y = [y_all[:, 2 * h * HG_DK:(2 * h + 1) * HG_DK] + y_all[:, (2 * h + 1) * HG_DK:(2 * h + 2) * HG_DK]
             for h in heads]
        b = [y[h][0:ch] for h in heads]
        b_end = [b[h][0:1] if reverse else b[h][ch - 1:ch] for h in heads]

        a = []
        for h in heads:
            ah = None
            for l in range(n_lev):
                ref = y[h][ch + l * nsb:ch + (l + 1) * nsb]
                el = jnp.exp2(-jnp.abs(b[h] - jnp.concatenate([ref] * HG_SUB, axis=0)))
                al = lax.dot_general((q[h] * el).astype(BF16), (kk[h] * el).astype(BF16), NT_DIMS,
                                     preferred_element_type=F32) * mask_ref[l]
                ah = al if ah is None else ah + al
            a.append(ah.astype(BF16))

        pieces = []
        for h in heads:
            for t, s in pairs:
                p = slab(q[h], t) * slab(kk[h], s)
                if s != t:
                    p = p * jnp.exp2(slab(b[h], t) - slab(b[h], s))
                pieces.append(p)
        rsum = jnp.dot(jnp.concatenate(pieces, axis=0).astype(BF16), ones, preferred_element_type=F32)

        outs = []
        for h in heads:
            cols = slice(h * HG_DK, (h + 1) * HG_DK)
            od = [None] * HG_SUB
            for n, (t, s) in enumerate(pairs):
                n0 = (h * len(pairs) + n) * nsb
                term = rsum[n0:n0 + nsb] * slab(v[h], s)
                od[t] = term if od[t] is None else od[t] + term
            qs = (q[h] * jnp.exp2(b[h])).astype(BF16)
            ks = (kk[h] * jnp.exp2(b_end[h] - b[h])).astype(BF16)
            st = st_ref[h]
            o = (jnp.dot(a[h], vb[h], preferred_element_type=F32) + jnp.concatenate(od, axis=0)
                 + lax.dot_general(qs, st.astype(BF16), NT_DIMS, preferred_element_type=F32))
            st_ref[h] = (st * jnp.exp2(b_end[h])
                         + lax.dot_general(vb[h], ks, TN_DIMS, preferred_element_type=F32))
            if final:
                o = o + of_ref[0, rows, cols]
                o = _rms(o, gw_ref[...], NORM_EPS) * _silu_exp2(col(3, h))
                outs.append(o.astype(BF16))
            else:
                o_ref[0, rows, cols] = o
        if final:
            y_out = jnp.dot(permt_ref[...], jnp.concatenate(outs, axis=1), preferred_element_type=F32)
            o_ref[0, rows, :] = y_out.astype(o_ref.dtype)
        return carry

    lax.fori_loop(0, n_chunks, chunk_body, 0, unroll=2)


def _hgrn_call(proj, lb_dir, gnorm_w, o_fwd, *, reverse):
    b, t, _ = proj.shape
    final = o_fwd is not None
    ch = HG_CHUNK
    tile = min(HG_TILE, t)
    assert t % tile == 0 and tile % ch == 0
    nt = t // tile
    perm_np, w_np, mask_np = _hgrn_constants(ch, reverse)

    def tok(j):
        return (nt - 1 - j) if reverse else j

    def col_spec(group):
        return pl.BlockSpec((1, tile, GROUP), lambda i, j: (i, tok(j), group))

    def const_spec(shape):
        return pl.BlockSpec(shape, lambda i, j: (0,) * len(shape))

    in_specs = [col_spec(COL_HQ), col_spec(COL_HFB if reverse else COL_HFF), col_spec(COL_HI)]
    args = [proj, proj, proj]
    if final:
        in_specs += [col_spec(COL_HG), pl.BlockSpec((1, tile, HG_WIDTH), lambda i, j: (i, tok(j), 0))]
        args += [proj, o_fwd]
    in_specs.append(const_spec((1, HG_WIDTH)))
    args.append(lb_dir.reshape(1, HG_WIDTH))
    if final:
        in_specs.append(const_spec((1, HG_DK)))
        args.append(gnorm_w.reshape(1, HG_DK))
    in_specs.append(const_spec(perm_np.shape))
    args.append(jnp.asarray(perm_np, BF16))
    if final:
        in_specs.append(const_spec(perm_np.shape))
        args.append(jnp.asarray(perm_np.T, BF16))
    in_specs += [const_spec(w_np.shape), const_spec(mask_np.shape)]
    args += [jnp.asarray(w_np, BF16), jnp.asarray(mask_np, F32)]

    kern = functools.partial(_hgrn_kernel, reverse=reverse, final=final, ch=ch, n_chunks=tile // ch)
    return pl.pallas_call(
        kern,
        out_shape=jax.ShapeDtypeStruct((b, t, HG_WIDTH), BF16 if final else F32),
        grid=(b, nt),
        in_specs=in_specs,
        out_specs=pl.BlockSpec((1, tile, HG_WIDTH), lambda i, j: (i, tok(j), 0)),
        scratch_shapes=[pltpu.VMEM((HG_HEADS, HG_DK, HG_DK), F32)],
        compiler_params=pltpu.CompilerParams(
            dimension_semantics=("parallel", "arbitrary"), vmem_limit_bytes=VMEM_LIMIT_BYTES),
        name="hgrn2_bwd" if reverse else "hgrn2_fwd",
    )(*args)


def _attn_kernel(q_ref, k_ref, v_ref, lq1_ref, lk1_ref, lq2_ref, lk2_ref, sw_ref, o_ref,
                 *, tq, tkc, n_kv, n_sub, lambda_init):
    for u in range(n_sub):
        _attn_tile(q_ref, k_ref, v_ref, lq1_ref, lk1_ref, lq2_ref, lk2_ref, sw_ref, o_ref,
                   slice(u * tq, (u + 1) * tq), tq=tq, tkc=tkc, n_kv=n_kv, lambda_init=lambda_init)


def _attn_tile(q_ref, k_ref, v_ref, lq1_ref, lk1_ref, lq2_ref, lk2_ref, sw_ref, o_ref, q_rows,
               *, tq, tkc, n_kv, lambda_init):
    q = q_ref[0, q_rows, :]
    lane = lax.broadcasted_iota(jnp.int32, q.shape, 1)
    zero = jnp.zeros_like(q)
    qs = jnp.concatenate([jnp.where(lane < DA_HEAD_DIM, q, zero),
                          jnp.where(lane >= DA_HEAD_DIM, q, zero)], axis=0)

    def body(j, carry):
        m_prev, l_prev, acc = carry
        rows = pl.ds(pl.multiple_of(j * tkc, tkc), tkc)
        k = k_ref[0, rows, :]
        v = v_ref[0, rows, :]
        s = lax.dot_general(qs, k, NT_DIMS, preferred_element_type=F32)
        m_new = jnp.maximum(m_prev, jnp.max(s, axis=-1, keepdims=True))
        alpha = jnp.exp2(m_prev - m_new)
        p = jnp.exp2(s - m_new)
        l_new = alpha * l_prev + jnp.sum(p, axis=-1, keepdims=True)
        acc = alpha * acc + jnp.dot(p.astype(BF16), v, preferred_element_type=F32)
        return m_new, l_new, acc

    init = (jnp.full((2 * tq, 1), -jnp.inf, F32), jnp.zeros((2 * tq, 1), F32),
            jnp.zeros((2 * tq, 2 * DA_HEAD_DIM), F32))
    _, l_fin, acc = lax.fori_loop(0, n_kv, body, init, unroll=True)

    o = acc / l_fin
    lam = (jnp.exp(jnp.sum(lq1_ref[...] * lk1_ref[...], axis=-1, keepdims=True))
           - jnp.exp(jnp.sum(lq2_ref[...] * lk2_ref[...], axis=-1, keepdims=True)) + lambda_init)
    od = o[0:tq] - lam * o[tq:2 * tq]
    od = _rms(od, sw_ref[...], SUBLN_EPS) * (1.0 - lambda_init)
    o_ref[0, q_rows, :] = od.astype(o_ref.dtype)


def _attn_call(proj, lq1, lk1, lq2, lk2, subln_w, lambda_init):
    b, t, _ = proj.shape
    tq = min(ATT_TQ, t)
    tkc = min(ATT_TK, t)
    n_sub = ATT_SUB if t % (ATT_SUB * tq) == 0 else 1
    assert t % (n_sub * tq) == 0 and t % tkc == 0
    head_w = 2 * DA_HEAD_DIM
    blocks_per_group = GROUP // head_w

    def lam_spec():
        return pl.BlockSpec((1, DA_HEAD_DIM), lambda i, h, qi: (0, 0))

    kern = functools.partial(_attn_kernel, tq=tq, tkc=tkc, n_kv=t // tkc, n_sub=n_sub, lambda_init=lambda_init)
    return pl.pallas_call(
        kern,
        out_shape=jax.ShapeDtypeStruct((b, t, DA_WIDTH), BF16),
        grid=(b, DA_HEADS, t // (n_sub * tq)),
        in_specs=[
            pl.BlockSpec((1, n_sub * tq, head_w), lambda i, h, qi: (i, qi, COL_DQ * blocks_per_group + h)),
            pl.BlockSpec((1, t, head_w), lambda i, h, qi: (i, 0, COL_DK * blocks_per_group + h)),
            pl.BlockSpec((1, t, head_w), lambda i, h, qi: (i, 0, COL_DV * blocks_per_group + h)),
            lam_spec(), lam_spec(), lam_spec(), lam_spec(),
            pl.BlockSpec((1, head_w), lambda i, h, qi: (0, 0)),
        ],
        out_specs=pl.BlockSpec((1, n_sub * tq, head_w), lambda i, h, qi: (i, qi, h)),
        compiler_params=pltpu.CompilerParams(
            dimension_semantics=("parallel", "parallel", "arbitrary"),
            vmem_limit_bytes=VMEM_LIMIT_BYTES),
        name="diff_attn",
    )(proj, proj, proj, lq1.reshape(1, -1), lk1.reshape(1, -1), lq2.reshape(1, -1), lk2.reshape(1, -1),
      subln_w.reshape(1, -1))


def _outffn_kernel(x_ref, ohg_ref, oda_ref, mod_ref, nw_ref, wo_ref, wg_ref, wu_ref, wd_ref, o_ref):
    x = x_ref[0]
    mix = (jnp.dot(ohg_ref[0], wo_ref[0:HG_WIDTH, :], preferred_element_type=F32)
           + jnp.dot(oda_ref[0], wo_ref[HG_WIDTH:HG_WIDTH + DA_WIDTH, :], preferred_element_type=F32))
    x1 = x + mod_ref[0, 2:3, :] * _rms(mix, nw_ref[0:1, :], NORM_EPS)
    h = _rms(x1, nw_ref[1:2, :], NORM_EPS) * (1.0 + mod_ref[0, 4:5, :]) + mod_ref[0, 3:4, :]
    h = h.astype(BF16)
    hidden = wg_ref.shape[1]
    f = None
    for c in range(hidden // FFN_CHUNK):
        cs = slice(c * FFN_CHUNK, (c + 1) * FFN_CHUNK)
        g = jnp.dot(h, wg_ref[:, cs], preferred_element_type=F32)
        u = jnp.dot(h, wu_ref[:, cs], preferred_element_type=F32)
        a = (_silu_exp2(g) * u).astype(BF16)
        fc = jnp.dot(a, wd_ref[cs, :], preferred_element_type=F32)
        f = fc if f is None else f + fc
    o_ref[0] = x1 + mod_ref[0, 5:6, :] * _rms(f, nw_ref[2:3, :], NORM_EPS)


def _outffn_call(x, o_hg, o_da, mod, norms, wo, wg, wu, wd):
    b, t, d = x.shape
    hidden = wg.shape[1]
    tm = min(TM_FFN, t)
    assert t % tm == 0 and hidden % FFN_CHUNK == 0

    def resident(shape):
        return pl.BlockSpec(shape, lambda i, j: (0, 0), pipeline_mode=pl.Buffered(1))

    return pl.pallas_call(
        _outffn_kernel,
        out_shape=jax.ShapeDtypeStruct((b, t, d), F32),
        grid=(b, t // tm),
        in_specs=[
            pl.BlockSpec((1, tm, d), lambda i, j: (i, j, 0)),
            pl.BlockSpec((1, tm, HG_WIDTH), lambda i, j: (i, j, 0)),
            pl.BlockSpec((1, tm, DA_WIDTH), lambda i, j: (i, j, 0)),
            pl.BlockSpec((1, N_MOD, d), lambda i, j: (i, 0, 0)),
            pl.BlockSpec((3, d), lambda i, j: (0, 0)),
            resident((HG_WIDTH + DA_WIDTH, d)),
            resident((d, hidden)),
            resident((d, hidden)),
            resident((hidden, d)),
        ],
        out_specs=pl.BlockSpec((1, tm, d), lambda i, j: (i, j, 0)),
        compiler_params=pltpu.CompilerParams(
            dimension_semantics=("parallel", "parallel"), vmem_limit_bytes=VMEM_LIMIT_BYTES),
        name="outproj_ffn",
    )(x, o_hg, o_da, mod, norms, wo, wg, wu, wd)


def _rotary_tables(t):
    dh = DA_HEAD_DIM
    inv = 1.0 / (ROPE_THETA ** (jnp.arange(0, dh, 2, dtype=F32) / dh))
    ang = jnp.arange(t, dtype=F32)[:, None] * inv[None, :]
    cos = jnp.cos(ang)
    sin = jnp.sin(ang)
    reps = LANES // dh
    cos_t = jnp.tile(jnp.concatenate([cos, cos], axis=-1), (1, reps))
    sin_t = jnp.tile(jnp.concatenate([-sin, sin], axis=-1), (1, reps))
    return cos_t, sin_t


def kernel(x_prompt, x_sample, c_prompt, c_sample, w_ada, b_ada, norm_pre_mix, norm_post_mix, norm_pre_ffn, norm_post_ffn, w_in, hg_lower_bounds, hg_gnorm, da_lambda_q1, da_lambda_k1, da_lambda_q2, da_lambda_k2, da_subln, w_out, w_ffn_gate, w_ffn_up, w_ffn_down):
    depth = w_in.shape[0]
    d = x_prompt.shape[-1]

    lb_soft = jax.nn.softmax(hg_lower_bounds.astype(F32), axis=1)
    lb_all = jnp.cumsum(lb_soft, axis=1) - lb_soft[:, :1]

    c_all = jnp.concatenate([c_prompt, c_sample], axis=0)
    mod_all = _ada_call(c_all, w_ada, b_ada)

    w_in_b = w_in.astype(BF16)
    w_out_b = w_out.astype(BF16)
    wg_b = w_ffn_gate.astype(BF16)
    wu_b = w_ffn_up.astype(BF16)
    wd_b = w_ffn_down.astype(BF16)

    def trunk(x, row0):
        b, t, _ = x.shape
        cos, sin = _rotary_tables(t)
        for l in range(depth):
            mod = mod_all[l, row0:row0 + b].reshape(b, N_MOD, d)
            lambda_init = 0.8 - 0.6 * math.exp(-0.3 * l)
            proj = _inproj_call(x, mod, norm_pre_mix[l], w_in_b[l], cos, sin)
            o_f = _hgrn_call(proj, lb_all[0, l], None, None, reverse=False)
            o_hg = _hgrn_call(proj, lb_all[1, l], hg_gnorm[l], o_f, reverse=True)
            o_da = _attn_call(proj, da_lambda_q1[l], da_lambda_k1[l], da_lambda_q2[l], da_lambda_k2[l],
                              da_subln[l], lambda_init)
            norms = jnp.stack([norm_post_mix[l], norm_pre_ffn[l], norm_post_ffn[l]], axis=0)
            x = _outffn_call(x, o_hg, o_da, mod, norms, w_out_b[l], wg_b[l], wu_b[l], wd_b[l])
        return x

    y_prompt = trunk(x_prompt, 0)
    y_sample = trunk(x_sample, c_prompt.shape[0])
    return (y_prompt, y_sample)
```

```python
import functools
import math

import numpy as np
import jax
import jax.numpy as jnp
from jax import lax
from jax.experimental import pallas as pl
from jax.experimental.pallas import tpu as pltpu

F32 = jnp.float32
BF16 = jnp.bfloat16

HG_WIDTH = 512
HG_HEADS = 4
HG_DK = 128
DA_WIDTH = 512
DA_HEADS = 4
DA_HEAD_DIM = 64
ROPE_THETA = 10000.0
NORM_EPS = 1e-6
SUBLN_EPS = 1e-5
LOG_FLOOR = 1e-30
LOG2_E = math.log2(math.e)
N_MOD = 6

LANES = 128
SUBLANES = 8
VMEM_LIMIT_BYTES = 56 * 1024 * 1024

COL_HQ, COL_HFF, COL_HFB, COL_HI, COL_HG, COL_DQ, COL_DK, COL_DV = range(8)
GROUP = 512

TM_PROJ = 512
TM_FFN = 512
FFN_CHUNK = 256
HG_CHUNK = 128
HG_TILE = 512
HG_SUB = SUBLANES
HG_ROWS = 2
HG_UNROLL = 2
ATT_TQ = 512
ATT_TK = 2048
ATT_SUB = 2

NT_DIMS = (((1,), (1,)), ((), ()))
TN_DIMS = (((0,), (0,)), ((), ()))


def _silu(x):
    return x * jax.nn.sigmoid(x)


def _rms(x, w, eps):
    ms = jnp.mean(x * x, axis=-1, keepdims=True)
    return x * lax.rsqrt(ms + eps) * w


def _ada_kernel(c_ref, w_ref, b_ref, o_ref):
    s = _silu(c_ref[...]).astype(BF16)
    w = w_ref[0].astype(BF16)
    o_ref[0] = jnp.dot(s, w, preferred_element_type=F32) + b_ref[0]


def _ada_call(c_all, w_ada, b_ada):
    depth, d, n = w_ada.shape
    rows = c_all.shape[0]
    tn = 1536
    assert n % tn == 0
    return pl.pallas_call(
        _ada_kernel,
        out_shape=jax.ShapeDtypeStruct((depth, rows, n), F32),
        grid=(depth, n // tn),
        in_specs=[
            pl.BlockSpec((rows, d), lambda l, j: (0, 0)),
            pl.BlockSpec((1, d, tn), lambda l, j: (l, 0, j)),
            pl.BlockSpec((1, 1, tn), lambda l, j: (l, 0, j)),
        ],
        out_specs=pl.BlockSpec((1, rows, tn), lambda l, j: (l, 0, j)),
        compiler_params=pltpu.CompilerParams(
            dimension_semantics=("parallel", "parallel"), vmem_limit_bytes=VMEM_LIMIT_BYTES),
        name="adaln_mod",
    )(c_all, w_ada, b_ada.reshape(depth, 1, n))


def _inproj_kernel(x_ref, mod_ref, nw_ref, w_ref, cos_ref, sin_ref, o_ref):
    x = x_ref[0]
    h = _rms(x, nw_ref[...], NORM_EPS) * (1.0 + mod_ref[0, 1:2, :]) + mod_ref[0, 0:1, :]
    h = h.astype(BF16)
    cos = cos_ref[...]
    sin = sin_ref[...]
    lane = lax.broadcasted_iota(jnp.int32, cos.shape, 1)
    first_half = (lane % DA_HEAD_DIM) < (DA_HEAD_DIM // 2)
    n_groups = w_ref.shape[1] // GROUP
    for n in range(n_groups):
        y = jnp.dot(h, w_ref[:, n * GROUP:(n + 1) * GROUP], preferred_element_type=F32)
        if n in (COL_DQ, COL_DK):
            pieces = []
            for j in range(GROUP // LANES):
                yj = y[:, j * LANES:(j + 1) * LANES]
                partner = jnp.where(first_half,
                                    pltpu.roll(yj, LANES - DA_HEAD_DIM // 2, 1),
                                    pltpu.roll(yj, DA_HEAD_DIM // 2, 1))
                r = yj * cos + partner * sin
                if n == COL_DQ:
                    r = r * (DA_HEAD_DIM ** -0.5 * LOG2_E)
                pieces.append(r)
            y = jnp.concatenate(pieces, axis=1)
        o_ref[0, :, n * GROUP:(n + 1) * GROUP] = y.astype(BF16)


def _inproj_call(x, mod, nw, w_bf16, cos, sin):
    b, t, d = x.shape
    n = w_bf16.shape[1]
    tm = min(TM_PROJ, t)
    assert t % tm == 0
    return pl.pallas_call(
        _inproj_kernel,
        out_shape=jax.ShapeDtypeStruct((b, t, n), BF16),
        grid=(b, t // tm),
        in_specs=[
            pl.BlockSpec((1, tm, d), lambda i, j: (i, j, 0)),
            pl.BlockSpec((1, N_MOD, d), lambda i, j: (i, 0, 0)),
            pl.BlockSpec((1, d), lambda i, j: (0, 0)),
            pl.BlockSpec((d, n), lambda i, j: (0, 0), pipeline_mode=pl.Buffered(1)),
            pl.BlockSpec((tm, LANES), lambda i, j: (j, 0)),
            pl.BlockSpec((tm, LANES), lambda i, j: (j, 0)),
        ],
        out_specs=pl.BlockSpec((1, tm, n), lambda i, j: (i, j, 0)),
        compiler_params=pltpu.CompilerParams(
            dimension_semantics=("parallel", "parallel"), vmem_limit_bytes=VMEM_LIMIT_BYTES),
        name="norm_inproj",
    )(x, mod, nw.reshape(1, d), w_bf16, cos, sin)


def _hgrn_levels(ch):
    levels = []
    m = HG_SUB
    while m < ch:
        levels.append(m)
        m *= 2
    return levels


@functools.lru_cache(maxsize=None)
def _hgrn_constants(ch, reverse):
    idx = np.arange(ch)
    nsb = ch // HG_SUB
    order = (idx % nsb) * HG_SUB + idx // nsb
    if reverse:
        cum = (idx[None, :] >= idx[:, None]).astype(np.float32)
    else:
        cum = (idx[None, :] <= idx[:, None]).astype(np.float32)
    refs = []
    masks = []
    for m in _hgrn_levels(ch):
        pair = idx // (2 * m)
        in_second = (idx % (2 * m)) >= m
        ref = pair * 2 * m + (m if reverse else m - 1)
        refs.append(cum[ref[::HG_SUB]])
        later = ~in_second if reverse else in_second
        earlier = ~later
        masks.append(((pair[:, None] == pair[None, :]) & later[:, None] & earlier[None, :]).astype(np.float32))
    sel = np.ix_(order, order)
    w = np.concatenate([cum[sel]] + [rf[:, order] for rf in refs], axis=0)
    masks = np.stack([mk[sel] for mk in masks], axis=0)
    perm = np.eye(ch, dtype=np.float32)[order]
    return perm, w, masks


def _silu_exp2(x):
    return x / (1.0 + jnp.exp2(-LOG2_E * x))


def _hgrn_kernel(*refs, reverse, final, ch, n_chunks):
    if final:
        (hq_ref, hf_ref, hi_ref, hg_ref, of_ref, lb_ref, gw_ref, perm_ref, permt_ref, w_ref, mask_ref,
         o_ref, st_ref) = refs
    else:
        (hq_ref, hf_ref, hi_ref, lb_ref, perm_ref, w_ref, mask_ref, o_ref, st_ref) = refs
    n_lev = len(_hgrn_levels(ch))
    nsb = ch // HG_SUB

    @pl.when(pl.program_id(1) == 0)
    def _():
        st_ref[...] = jnp.zeros_like(st_ref)

    lb = lb_ref[...]
    lb_floor = jnp.maximum(lb, LOG_FLOOR)
    one_m_lb = 1.0 - lb
    ones = jnp.ones((LANES, LANES), BF16)

    def slab(x, t):
        return x[t * nsb:(t + 1) * nsb]

    nb = hq_ref.shape[0]
    n_groups = 4 if final else 3
    units = [(bi, h) for bi in range(nb) for h in range(HG_HEADS)]
    pairs = [(t, s) for t in range(HG_SUB) for s in range(HG_SUB) if ((s >= t) if reverse else (s <= t))]

    def chunk_body(i, carry):
        c = (n_chunks - 1 - i) if reverse else i
        rows = pl.ds(pl.multiple_of(c * ch, ch), ch)
        parts = []
        for bi in range(nb):
            parts += [hq_ref[bi, rows, :], hf_ref[bi, rows, :], hi_ref[bi, rows, :]]
            if final:
                parts.append(hg_ref[bi, rows, :])
        xp = jnp.dot(perm_ref[...], jnp.concatenate(parts, axis=1), preferred_element_type=F32)

        def col(group, bi, h):
            c0 = (bi * n_groups + group) * HG_WIDTH + h * HG_DK
            return xp[:, c0:c0 + HG_DK]

        q, kk, v, vb, g_split = [], [], [], [], []
        for bi, h in units:
            cols = slice(h * HG_DK, (h + 1) * HG_DK)
            z = col(1, bi, h)
            e = jnp.exp2(-LOG2_E * jnp.abs(z))
            r = 1.0 / (1.0 + e)
            er = e * r
            pos = z >= 0.0
            g = jnp.log2(lb_floor[:, cols] + one_m_lb[:, cols] * jnp.where(pos, r, er))
            kk.append(one_m_lb[:, cols] * jnp.where(pos, er, r))
            q.append(_silu_exp2(col(0, bi, h)))
            v.append(col(2, bi, h))
            vb.append(col(2, bi, h).astype(BF16))
            g_hi = g.astype(BF16)
            g_split += [g_hi, (g - g_hi.astype(F32)).astype(BF16)]

        y_all = jnp.dot(w_ref[...], jnp.concatenate(g_split, axis=1), preferred_element_type=F32)
        y = [y_all[:, 2 * u * HG_DK:(2 * u + 1) * HG_DK] + y_all[:, (2 * u + 1) * HG_DK:(2 * u + 2) * HG_DK]
             for u in range(len(units))]
        b = [yu[0:ch] for yu in y]
        b_end = [bu[0:1] if reverse else bu[ch - 1:ch] for bu in b]

        a = []
        for u in range(len(units)):
            au = None
            for l in range(n_lev):
                ref = y[u][ch + l * nsb:ch + (l + 1) * nsb]
                el = jnp.exp2(-jnp.abs(b[u] - jnp.concatenate([ref] * HG_SUB, axis=0)))
                al = lax.dot_general((q[u] * el).astype(BF16), (kk[u] * el).astype(BF16), NT_DIMS,
                                     preferred_element_type=F32) * mask_ref[l]
                au = al if au is None else au + al
            a.append(au.astype(BF16))

        pieces = []
        for u in range(len(units)):
            for t, s in pairs:
                p = slab(q[u], t) * slab(kk[u], s)
                if s != t:
                    p = p * jnp.exp2(slab(b[u], t) - slab(b[u], s))
                pieces.append(p)
        rsum = jnp.dot(jnp.concatenate(pieces, axis=0).astype(BF16), ones, preferred_element_type=F32)

        outs = []
        for u, (bi, h) in enumerate(units):
            cols = slice(h * HG_DK, (h + 1) * HG_DK)
            od = [None] * HG_SUB
            for n, (t, s) in enumerate(pairs):
                n0 = (u * len(pairs) + n) * nsb
                term = rsum[n0:n0 + nsb] * slab(v[u], s)
                od[t] = term if od[t] is None else od[t] + term
            qs = (q[u] * jnp.exp2(b[u])).astype(BF16)
            ks = (kk[u] * jnp.exp2(b_end[u] - b[u])).astype(BF16)
            st = st_ref[bi, h]
            o = (jnp.dot(a[u], vb[u], preferred_element_type=F32) + jnp.concatenate(od, axis=0)
                 + lax.dot_general(qs, st.astype(BF16), NT_DIMS, preferred_element_type=F32))
            st_ref[bi, h] = (st * jnp.exp2(b_end[u])
                             + lax.dot_general(vb[u], ks, TN_DIMS, preferred_element_type=F32))
            if final:
                o = o + of_ref[bi, rows, cols]
                o = _rms(o, gw_ref[...], NORM_EPS) * _silu_exp2(col(3, bi, h))
                outs.append(o.astype(BF16))
            else:
                o_ref[bi, rows, cols] = o
        if final:
            y_out = jnp.dot(permt_ref[...], jnp.concatenate(outs, axis=1), preferred_element_type=F32)
            for bi in range(nb):
                o_ref[bi, rows, :] = y_out[:, bi * HG_WIDTH:(bi + 1) * HG_WIDTH].astype(o_ref.dtype)
        return carry

    lax.fori_loop(0, n_chunks, chunk_body, 0, unroll=HG_UNROLL)


def _hgrn_call(proj, lb_dir, gnorm_w, o_fwd, *, reverse):
    b, t, _ = proj.shape
    final = o_fwd is not None
    ch = HG_CHUNK
    tile = min(HG_TILE, t)
    assert t % tile == 0 and tile % ch == 0
    nt = t // tile
    perm_np, w_np, mask_np = _hgrn_constants(ch, reverse)

    def tok(j):
        return (nt - 1 - j) if reverse else j

    nb = HG_ROWS if b % HG_ROWS == 0 else 1

    def col_spec(group):
        return pl.BlockSpec((nb, tile, GROUP), lambda i, j: (i, tok(j), group))

    def const_spec(shape):
        return pl.BlockSpec(shape, lambda i, j: (0,) * len(shape))

    in_specs = [col_spec(COL_HQ), col_spec(COL_HFB if reverse else COL_HFF), col_spec(COL_HI)]
    args = [proj, proj, proj]
    if final:
        in_specs += [col_spec(COL_HG), pl.BlockSpec((nb, tile, HG_WIDTH), lambda i, j: (i, tok(j), 0))]
        args += [proj, o_fwd]
    in_specs.append(const_spec((1, HG_WIDTH)))
    args.append(lb_dir.reshape(1, HG_WIDTH))
    if final:
        in_specs.append(const_spec((1, HG_DK)))
        args.append(gnorm_w.reshape(1, HG_DK))
    in_specs.append(const_spec(perm_np.shape))
    args.append(jnp.asarray(perm_np, BF16))
    if final:
        in_specs.append(const_spec(perm_np.shape))
        args.append(jnp.asarray(perm_np.T, BF16))
    in_specs += [const_spec(w_np.shape), const_spec(mask_np.shape)]
    args += [jnp.asarray(w_np, BF16), jnp.asarray(mask_np, F32)]

    kern = functools.partial(_hgrn_kernel, reverse=reverse, final=final, ch=ch, n_chunks=tile // ch)
    return pl.pallas_call(
        kern,
        out_shape=jax.ShapeDtypeStruct((b, t, HG_WIDTH), BF16 if final else F32),
        grid=(b // nb, nt),
        in_specs=in_specs,
        out_specs=pl.BlockSpec((nb, tile, HG_WIDTH), lambda i, j: (i, tok(j), 0)),
        scratch_shapes=[pltpu.VMEM((nb, HG_HEADS, HG_DK, HG_DK), F32)],
        compiler_params=pltpu.CompilerParams(
            dimension_semantics=("parallel", "arbitrary"), vmem_limit_bytes=VMEM_LIMIT_BYTES),
        name="hgrn2_bwd" if reverse else "hgrn2_fwd",
    )(*args)


def _attn_kernel(q_ref, k_ref, v_ref, lq1_ref, lk1_ref, lq2_ref, lk2_ref, sw_ref, o_ref,
                 *, tq, tkc, n_kv, n_sub, lambda_init):
    for u in range(n_sub):
        _attn_tile(q_ref, k_ref, v_ref, lq1_ref, lk1_ref, lq2_ref, lk2_ref, sw_ref, o_ref,
                   slice(u * tq, (u + 1) * tq), tq=tq, tkc=tkc, n_kv=n_kv, lambda_init=lambda_init)


def _attn_tile(q_ref, k_ref, v_ref, lq1_ref, lk1_ref, lq2_ref, lk2_ref, sw_ref, o_ref, q_rows,
               *, tq, tkc, n_kv, lambda_init):
    q = q_ref[0, q_rows, :]
    lane = lax.broadcasted_iota(jnp.int32, q.shape, 1)
    zero = jnp.zeros_like(q)
    qs = jnp.concatenate([jnp.where(lane < DA_HEAD_DIM, q, zero),
                          jnp.where(lane >= DA_HEAD_DIM, q, zero)], axis=0)

    def body(j, carry):
        m_prev, l_prev, acc = carry
        rows = pl.ds(pl.multiple_of(j * tkc, tkc), tkc)
        k = k_ref[0, rows, :]
        v = v_ref[0, rows, :]
        s = lax.dot_general(qs, k, NT_DIMS, preferred_element_type=F32)
        m_new = jnp.maximum(m_prev, jnp.max(s, axis=-1, keepdims=True))
        alpha = jnp.exp2(m_prev - m_new)
        p = jnp.exp2(s - m_new)
        l_new = alpha * l_prev + jnp.sum(p, axis=-1, keepdims=True)
        acc = alpha * acc + jnp.dot(p.astype(BF16), v, preferred_element_type=F32)
        return m_new, l_new, acc

    init = (jnp.full((2 * tq, 1), -jnp.inf, F32), jnp.zeros((2 * tq, 1), F32),
            jnp.zeros((2 * tq, 2 * DA_HEAD_DIM), F32))
    _, l_fin, acc = lax.fori_loop(0, n_kv, body, init, unroll=True)

    o = acc / l_fin
    lam = (jnp.exp(jnp.sum(lq1_ref[...] * lk1_ref[...], axis=-1, keepdims=True))
           - jnp.exp(jnp.sum(lq2_ref[...] * lk2_ref[...], axis=-1, keepdims=True)) + lambda_init)
    od = o[0:tq] - lam * o[tq:2 * tq]
    od = _rms(od, sw_ref[...], SUBLN_EPS) * (1.0 - lambda_init)
    o_ref[0, q_rows, :] = od.astype(o_ref.dtype)


def _attn_call(proj, lq1, lk1, lq2, lk2, subln_w, lambda_init):
    b, t, _ = proj.shape
    tq = min(ATT_TQ, t)
    tkc = min(ATT_TK, t)
    n_sub = ATT_SUB if t % (ATT_SUB * tq) == 0 else 1
    assert t % (n_sub * tq) == 0 and t % tkc == 0
    head_w = 2 * DA_HEAD_DIM
    blocks_per_group = GROUP // head_w

    def lam_spec():
        return pl.BlockSpec((1, DA_HEAD_DIM), lambda i, h, qi: (0, 0))

    kern = functools.partial(_attn_kernel, tq=tq, tkc=tkc, n_kv=t // tkc, n_sub=n_sub, lambda_init=lambda_init)
    return pl.pallas_call(
        kern,
        out_shape=jax.ShapeDtypeStruct((b, t, DA_WIDTH), BF16),
        grid=(b, DA_HEADS, t // (n_sub * tq)),
        in_specs=[
            pl.BlockSpec((1, n_sub * tq, head_w), lambda i, h, qi: (i, qi, COL_DQ * blocks_per_group + h)),
            pl.BlockSpec((1, t, head_w), lambda i, h, qi: (i, 0, COL_DK * blocks_per_group + h)),
            pl.BlockSpec((1, t, head_w), lambda i, h, qi: (i, 0, COL_DV * blocks_per_group + h)),
            lam_spec(), lam_spec(), lam_spec(), lam_spec(),
            pl.BlockSpec((1, head_w), lambda i, h, qi: (0, 0)),
        ],
        out_specs=pl.BlockSpec((1, n_sub * tq, head_w), lambda i, h, qi: (i, qi, h)),
        compiler_params=pltpu.CompilerParams(
            dimension_semantics=("parallel", "parallel", "arbitrary"),
            vmem_limit_bytes=VMEM_LIMIT_BYTES),
        name="diff_attn",
    )(proj, proj, proj, lq1.reshape(1, -1), lk1.reshape(1, -1), lq2.reshape(1, -1), lk2.reshape(1, -1),
      subln_w.reshape(1, -1))


def _outffn_kernel(x_ref, ohg_ref, oda_ref, mod_ref, nw_ref, wo_ref, wg_ref, wu_ref, wd_ref, o_ref):
    x = x_ref[0]
    mix = (jnp.dot(ohg_ref[0], wo_ref[0:HG_WIDTH, :], preferred_element_type=F32)
           + jnp.dot(oda_ref[0], wo_ref[HG_WIDTH:HG_WIDTH + DA_WIDTH, :], preferred_element_type=F32))
    x1 = x + mod_ref[0, 2:3, :] * _rms(mix, nw_ref[0:1, :], NORM_EPS)
    h = _rms(x1, nw_ref[1:2, :], NORM_EPS) * (1.0 + mod_ref[0, 4:5, :]) + mod_ref[0, 3:4, :]
    h = h.astype(BF16)
    hidden = wg_ref.shape[1]
    f = None
    for c in range(hidden // FFN_CHUNK):
        cs = slice(c * FFN_CHUNK, (c + 1) * FFN_CHUNK)
        g = jnp.dot(h, wg_ref[:, cs], preferred_element_type=F32)
        u = jnp.dot(h, wu_ref[:, cs], preferred_element_type=F32)
        a = (_silu_exp2(g) * u).astype(BF16)
        fc = jnp.dot(a, wd_ref[cs, :], preferred_element_type=F32)
        f = fc if f is None else f + fc
    o_ref[0] = x1 + mod_ref[0, 5:6, :] * _rms(f, nw_ref[2:3, :], NORM_EPS)


def _outffn_call(x, o_hg, o_da, mod, norms, wo, wg, wu, wd):
    b, t, d = x.shape
    hidden = wg.shape[1]
    tm = min(TM_FFN, t)
    assert t % tm == 0 and hidden % FFN_CHUNK == 0

    def resident(shape):
        return pl.BlockSpec(shape, lambda i, j: (0, 0), pipeline_mode=pl.Buffered(1))

    return pl.pallas_call(
        _outffn_kernel,
        out_shape=jax.ShapeDtypeStruct((b, t, d), F32),
        grid=(b, t // tm),
        in_specs=[
            pl.BlockSpec((1, tm, d), lambda i, j: (i, j, 0)),
            pl.BlockSpec((1, tm, HG_WIDTH), lambda i, j: (i, j, 0)),
            pl.BlockSpec((1, tm, DA_WIDTH), lambda i, j: (i, j, 0)),
            pl.BlockSpec((1, N_MOD, d), lambda i, j: (i, 0, 0)),
            pl.BlockSpec((3, d), lambda i, j: (0, 0)),
            resident((HG_WIDTH + DA_WIDTH, d)),
            resident((d, hidden)),
            resident((d, hidden)),
            resident((hidden, d)),
        ],
        out_specs=pl.BlockSpec((1, tm, d), lambda i, j: (i, j, 0)),
        compiler_params=pltpu.CompilerParams(
            dimension_semantics=("parallel", "parallel"), vmem_limit_bytes=VMEM_LIMIT_BYTES),
        name="outproj_ffn",
    )(x, o_hg, o_da, mod, norms, wo, wg, wu, wd)


def _rotary_tables(t):
    dh = DA_HEAD_DIM
    inv = 1.0 / (ROPE_THETA ** (jnp.arange(0, dh, 2, dtype=F32) / dh))
    ang = jnp.arange(t, dtype=F32)[:, None] * inv[None, :]
    cos = jnp.cos(ang)
    sin = jnp.sin(ang)
    reps = LANES // dh
    cos_t = jnp.tile(jnp.concatenate([cos, cos], axis=-1), (1, reps))
    sin_t = jnp.tile(jnp.concatenate([-sin, sin], axis=-1), (1, reps))
    return cos_t, sin_t


def kernel(x_prompt, x_sample, c_prompt, c_sample, w_ada, b_ada, norm_pre_mix, norm_post_mix, norm_pre_ffn, norm_post_ffn, w_in, hg_lower_bounds, hg_gnorm, da_lambda_q1, da_lambda_k1, da_lambda_q2, da_lambda_k2, da_subln, w_out, w_ffn_gate, w_ffn_up, w_ffn_down):
    depth = w_in.shape[0]
    d = x_prompt.shape[-1]

    lb_soft = jax.nn.softmax(hg_lower_bounds.astype(F32), axis=1)
    lb_all = jnp.cumsum(lb_soft, axis=1) - lb_soft[:, :1]

    c_all = jnp.concatenate([c_prompt, c_sample], axis=0)
    mod_all = _ada_call(c_all, w_ada, b_ada)

    w_in_b = w_in.astype(BF16)
    w_out_b = w_out.astype(BF16)
    wg_b = w_ffn_gate.astype(BF16)
    wu_b = w_ffn_up.astype(BF16)
    wd_b = w_ffn_down.astype(BF16)

    def trunk(x, row0):
        b, t, _ = x.shape
        cos, sin = _rotary_tables(t)
        for l in range(depth):
            mod = mod_all[l, row0:row0 + b].reshape(b, N_MOD, d)
            lambda_init = 0.8 - 0.6 * math.exp(-0.3 * l)
            proj = _inproj_call(x, mod, norm_pre_mix[l], w_in_b[l], cos, sin)
            o_f = _hgrn_call(proj, lb_all[0, l], None, None, reverse=False)
            o_hg = _hgrn_call(proj, lb_all[1, l], hg_gnorm[l], o_f, reverse=True)
            o_da = _attn_call(proj, da_lambda_q1[l], da_lambda_k1[l], da_lambda_q2[l], da_lambda_k2[l],
                              da_subln[l], lambda_init)
            norms = jnp.stack([norm_post_mix[l], norm_pre_ffn[l], norm_post_ffn[l]], axis=0)
            x = _outffn_call(x, o_hg, o_da, mod, norms, w_out_b[l], wg_b[l], wu_b[l], wd_b[l])
        return x

    y_prompt = trunk(x_prompt, 0)
    y_sample = trunk(x_sample, c_prompt.shape[0])
    return (y_prompt, y_sample)
```

```python
import functools
import math

import numpy as np
import jax
import jax.numpy as jnp
from jax import lax
from jax.experimental import pallas as pl
from jax.experimental.pallas import tpu as pltpu

F32 = jnp.float32
BF16 = jnp.bfloat16

HG_WIDTH = 512
HG_HEADS = 4
HG_DK = 128
DA_WIDTH = 512
DA_HEADS = 4
DA_HEAD_DIM = 64
ROPE_THETA = 10000.0
NORM_EPS = 1e-6
SUBLN_EPS = 1e-5
LOG_FLOOR = 1e-30
LOG2_E = math.log2(math.e)
N_MOD = 6

LANES = 128
SUBLANES = 8
VMEM_LIMIT_BYTES = 56 * 1024 * 1024

COL_HQ, COL_HFF, COL_HFB, COL_HI, COL_HG, COL_DQ, COL_DK, COL_DV = range(8)
GROUP = 512

TM_PROJ = 512
TM_FFN = 512
FFN_CHUNK = 256
HG_CHUNK = 128
HG_TILE = 512
HG_SUB = SUBLANES
HG_ROWS = 4
HG_UNROLL = 2
ATT_TQ = 512
ATT_TK = 2048
ATT_SUB = 2

NT_DIMS = (((1,), (1,)), ((), ()))
TN_DIMS = (((0,), (0,)), ((), ()))


def _silu(x):
    return x * jax.nn.sigmoid(x)


def _rms(x, w, eps):
    ms = jnp.mean(x * x, axis=-1, keepdims=True)
    return x * lax.rsqrt(ms + eps) * w


def _ada_kernel(c_ref, w_ref, b_ref, o_ref):
    s = _silu(c_ref[...]).astype(BF16)
    w = w_ref[0].astype(BF16)
    o_ref[0] = jnp.dot(s, w, preferred_element_type=F32) + b_ref[0]


def _ada_call(c_all, w_ada, b_ada):
    depth, d, n = w_ada.shape
    rows = c_all.shape[0]
    tn = 1536
    assert n % tn == 0
    return pl.pallas_call(
        _ada_kernel,
        out_shape=jax.ShapeDtypeStruct((depth, rows, n), F32),
        grid=(depth, n // tn),
        in_specs=[
            pl.BlockSpec((rows, d), lambda l, j: (0, 0)),
            pl.BlockSpec((1, d, tn), lambda l, j: (l, 0, j)),
            pl.BlockSpec((1, 1, tn), lambda l, j: (l, 0, j)),
        ],
        out_specs=pl.BlockSpec((1, rows, tn), lambda l, j: (l, 0, j)),
        compiler_params=pltpu.CompilerParams(
            dimension_semantics=("parallel", "parallel"), vmem_limit_bytes=VMEM_LIMIT_BYTES),
        name="adaln_mod",
    )(c_all, w_ada, b_ada.reshape(depth, 1, n))


def _inproj_kernel(x_ref, mod_ref, nw_ref, w_ref, cos_ref, sin_ref, o_ref):
    x = x_ref[0]
    h = _rms(x, nw_ref[...], NORM_EPS) * (1.0 + mod_ref[0, 1:2, :]) + mod_ref[0, 0:1, :]
    h = h.astype(BF16)
    cos = cos_ref[...]
    sin = sin_ref[...]
    lane = lax.broadcasted_iota(jnp.int32, cos.shape, 1)
    first_half = (lane % DA_HEAD_DIM) < (DA_HEAD_DIM // 2)
    n_groups = w_ref.shape[1] // GROUP
    for n in range(n_groups):
        y = jnp.dot(h, w_ref[:, n * GROUP:(n + 1) * GROUP], preferred_element_type=F32)
        if n in (COL_DQ, COL_DK):
            pieces = []
            for j in range(GROUP // LANES):
                yj = y[:, j * LANES:(j + 1) * LANES]
                partner = jnp.where(first_half,
                                    pltpu.roll(yj, LANES - DA_HEAD_DIM // 2, 1),
                                    pltpu.roll(yj, DA_HEAD_DIM // 2, 1))
                r = yj * cos + partner * sin
                if n == COL_DQ:
                    r = r * (DA_HEAD_DIM ** -0.5 * LOG2_E)
                pieces.append(r)
            y = jnp.concatenate(pieces, axis=1)
        o_ref[0, :, n * GROUP:(n + 1) * GROUP] = y.astype(BF16)


def _inproj_call(x, mod, nw, w_bf16, cos, sin):
    b, t, d = x.shape
    n = w_bf16.shape[1]
    tm = min(TM_PROJ, t)
    assert t % tm == 0
    return pl.pallas_call(
        _inproj_kernel,
        out_shape=jax.ShapeDtypeStruct((b, t, n), BF16),
        grid=(b, t // tm),
        in_specs=[
            pl.BlockSpec((1, tm, d), lambda i, j: (i, j, 0)),
            pl.BlockSpec((1, N_MOD, d), lambda i, j: (i, 0, 0)),
            pl.BlockSpec((1, d), lambda i, j: (0, 0)),
            pl.BlockSpec((d, n), lambda i, j: (0, 0), pipeline_mode=pl.Buffered(1)),
            pl.BlockSpec((tm, LANES), lambda i, j: (j, 0)),
            pl.BlockSpec((tm, LANES), lambda i, j: (j, 0)),
        ],
        out_specs=pl.BlockSpec((1, tm, n), lambda i, j: (i, j, 0)),
        compiler_params=pltpu.CompilerParams(
            dimension_semantics=("parallel", "parallel"), vmem_limit_bytes=VMEM_LIMIT_BYTES),
        name="norm_inproj",
    )(x, mod, nw.reshape(1, d), w_bf16, cos, sin)


def _hgrn_levels(ch):
    levels = []
    m = HG_SUB
    while m < ch:
        levels.append(m)
        m *= 2
    return levels


@functools.lru_cache(maxsize=None)
def _hgrn_constants(ch, reverse):
    idx = np.arange(ch)
    nsb = ch // HG_SUB
    order = (idx % nsb) * HG_SUB + idx // nsb
    if reverse:
        cum = (idx[None, :] >= idx[:, None]).astype(np.float32)
    else:
        cum = (idx[None, :] <= idx[:, None]).astype(np.float32)
    refs = []
    masks = []
    for m in _hgrn_levels(ch):
        pair = idx // (2 * m)
        in_second = (idx % (2 * m)) >= m
        ref = pair * 2 * m + (m if reverse else m - 1)
        refs.append(cum[ref[::HG_SUB]])
        later = ~in_second if reverse else in_second
        earlier = ~later
        masks.append(((pair[:, None] == pair[None, :]) & later[:, None] & earlier[None, :]).astype(np.float32))
    sel = np.ix_(order, order)
    w = np.concatenate([cum[sel]] + [rf[:, order] for rf in refs], axis=0)
    w = np.concatenate([w, w], axis=1)
    masks = np.stack([mk[sel] for mk in masks], axis=0)
    perm = np.eye(ch, dtype=np.float32)[order]
    return perm, w, masks


def _silu_exp2(x):
    return x / (1.0 + jnp.exp2(-LOG2_E * x))


def _hgrn_kernel(*refs, reverse, final, ch, n_chunks):
    if final:
        (hq_ref, hf_ref, hi_ref, hg_ref, of_ref, lb_ref, gw_ref, perm_ref, permt_ref, w_ref, mask_ref,
         o_ref, st_ref) = refs
    else:
        (hq_ref, hf_ref, hi_ref, lb_ref, perm_ref, w_ref, mask_ref, o_ref, st_ref) = refs
    n_lev = len(_hgrn_levels(ch))
    nsb = ch // HG_SUB

    @pl.when(pl.program_id(1) == 0)
    def _():
        st_ref[...] = jnp.zeros_like(st_ref)

    lb = lb_ref[...]
    lb_floor = jnp.maximum(lb, LOG_FLOOR)
    one_m_lb = 1.0 - lb
    ones = jnp.ones((LANES, LANES), BF16)

    def slab(x, t):
        return x[t * nsb:(t + 1) * nsb]

    nb = hq_ref.shape[0]
    n_groups = 4 if final else 3
    units = [(bi, h) for bi in range(nb) for h in range(HG_HEADS)]
    pairs = [(t, s) for t in range(HG_SUB) for s in range(HG_SUB) if ((s >= t) if reverse else (s <= t))]

    def chunk_body(i, carry):
        c = (n_chunks - 1 - i) if reverse else i
        rows = pl.ds(pl.multiple_of(c * ch, ch), ch)
        parts = []
        for bi in range(nb):
            parts += [hq_ref[bi, rows, :], hf_ref[bi, rows, :], hi_ref[bi, rows, :]]
            if final:
                parts.append(hg_ref[bi, rows, :])
        xp = jnp.dot(perm_ref[...], jnp.concatenate(parts, axis=1), preferred_element_type=F32)

        def col(group, bi, h):
            c0 = (bi * n_groups + group) * HG_WIDTH + h * HG_DK
            return xp[:, c0:c0 + HG_DK]

        q, kk, qb, kb, v, vb, g_hi, g_lo = [], [], [], [], [], [], [], []
        for bi, h in units:
            cols = slice(h * HG_DK, (h + 1) * HG_DK)
            z = col(1, bi, h)
            e = jnp.exp2(-LOG2_E * jnp.abs(z))
            r = 1.0 / (1.0 + e)
            er = e * r
            pos = z >= 0.0
            g = jnp.log2(lb_floor[:, cols] + one_m_lb[:, cols] * jnp.where(pos, r, er))
            kk.append(one_m_lb[:, cols] * jnp.where(pos, er, r))
            q.append(_silu_exp2(col(0, bi, h)))
            qb.append(q[-1].astype(BF16))
            kb.append(kk[-1].astype(BF16))
            v.append(col(2, bi, h))
            vb.append(col(2, bi, h).astype(BF16))
            g_hi.append(g.astype(BF16))
            g_lo.append((g - g_hi[-1].astype(F32)).astype(BF16))

        g2 = jnp.concatenate([jnp.concatenate(g_hi, axis=1), jnp.concatenate(g_lo, axis=1)], axis=0)
        y_all = jnp.dot(w_ref[...], g2, preferred_element_type=F32)
        y = [y_all[:, u * HG_DK:(u + 1) * HG_DK] for u in range(len(units))]
        b = [yu[0:ch] for yu in y]
        b_end = [bu[0:1] if reverse else bu[ch - 1:ch] for bu in b]

        a = []
        for u in range(len(units)):
            au = None
            for l in range(n_lev):
                ref = y[u][ch + l * nsb:ch + (l + 1) * nsb]
                el = jnp.exp2(-jnp.abs(b[u] - jnp.concatenate([ref] * HG_SUB, axis=0))).astype(BF16)
                al = lax.dot_general(qb[u] * el, kb[u] * el, NT_DIMS,
                                     preferred_element_type=F32) * mask_ref[l]
                au = al if au is None else au + al
            a.append(au.astype(BF16))

        pieces = []
        for u in range(len(units)):
            for t, s in pairs:
                p = slab(qb[u], t) * slab(kb[u], s)
                if s != t:
                    p = p * jnp.exp2(slab(b[u], t) - slab(b[u], s)).astype(BF16)
                pieces.append(p)
        rsum = jnp.dot(jnp.concatenate(pieces, axis=0), ones, preferred_element_type=F32)

        outs = []
        for u, (bi, h) in enumerate(units):
            cols = slice(h * HG_DK, (h + 1) * HG_DK)
            od = [None] * HG_SUB
            for n, (t, s) in enumerate(pairs):
                n0 = (u * len(pairs) + n) * nsb
                term = rsum[n0:n0 + nsb] * slab(v[u], s)
                od[t] = term if od[t] is None else od[t] + term
            qs = (q[u] * jnp.exp2(b[u])).astype(BF16)
            ks = (kk[u] * jnp.exp2(b_end[u] - b[u])).astype(BF16)
            st = st_ref[bi, h]
            o = (jnp.dot(a[u], vb[u], preferred_element_type=F32) + jnp.concatenate(od, axis=0)
                 + lax.dot_general(qs, st.astype(BF16), NT_DIMS, preferred_element_type=F32))
            st_ref[bi, h] = (st * jnp.exp2(b_end[u])
                             + lax.dot_general(vb[u], ks, TN_DIMS, preferred_element_type=F32))
            if final:
                o = o + of_ref[bi, rows, cols]
                o = _rms(o, gw_ref[...], NORM_EPS) * _silu_exp2(col(3, bi, h))
                outs.append(o.astype(BF16))
            else:
                o_ref[bi, rows, cols] = o
        if final:
            y_out = jnp.dot(permt_ref[...], jnp.concatenate(outs, axis=1), preferred_element_type=F32)
            for bi in range(nb):
                o_ref[bi, rows, :] = y_out[:, bi * HG_WIDTH:(bi + 1) * HG_WIDTH].astype(o_ref.dtype)
        return carry

    lax.fori_loop(0, n_chunks, chunk_body, 0, unroll=HG_UNROLL)


def _hgrn_call(proj, lb_dir, gnorm_w, o_fwd, *, reverse):
    b, t, _ = proj.shape
    final = o_fwd is not None
    ch = HG_CHUNK
    tile = min(HG_TILE, t)
    assert t % tile == 0 and tile % ch == 0
    nt = t // tile
    perm_np, w_np, mask_np = _hgrn_constants(ch, reverse)

    def tok(j):
        return (nt - 1 - j) if reverse else j

    nb = HG_ROWS if b % HG_ROWS == 0 else 1

    def col_spec(group):
        return pl.BlockSpec((nb, tile, GROUP), lambda i, j: (i, tok(j), group))

    def const_spec(shape):
        return pl.BlockSpec(shape, lambda i, j: (0,) * len(shape))

    in_specs = [col_spec(COL_HQ), col_spec(COL_HFB if reverse else COL_HFF), col_spec(COL_HI)]
    args = [proj, proj, proj]
    if final:
        in_specs += [col_spec(COL_HG), pl.BlockSpec((nb, tile, HG_WIDTH), lambda i, j: (i, tok(j), 0))]
        args += [proj, o_fwd]
    in_specs.append(const_spec((1, HG_WIDTH)))
    args.append(lb_dir.reshape(1, HG_WIDTH))
    if final:
        in_specs.append(const_spec((1, HG_DK)))
        args.append(gnorm_w.reshape(1, HG_DK))
    in_specs.append(const_spec(perm_np.shape))
    args.append(jnp.asarray(perm_np, BF16))
    if final:
        in_specs.append(const_spec(perm_np.shape))
        args.append(jnp.asarray(perm_np.T, BF16))
    in_specs += [const_spec(w_np.shape), const_spec(mask_np.shape)]
    args += [jnp.asarray(w_np, BF16), jnp.asarray(mask_np, F32)]

    kern = functools.partial(_hgrn_kernel, reverse=reverse, final=final, ch=ch, n_chunks=tile // ch)
    return pl.pallas_call(
        kern,
        out_shape=jax.ShapeDtypeStruct((b, t, HG_WIDTH), BF16 if final else F32),
        grid=(b // nb, nt),
        in_specs=in_specs,
        out_specs=pl.BlockSpec((nb, tile, HG_WIDTH), lambda i, j: (i, tok(j), 0)),
        scratch_shapes=[pltpu.VMEM((nb, HG_HEADS, HG_DK, HG_DK), F32)],
        compiler_params=pltpu.CompilerParams(
            dimension_semantics=("parallel", "arbitrary"), vmem_limit_bytes=VMEM_LIMIT_BYTES),
        name="hgrn2_bwd" if reverse else "hgrn2_fwd",
    )(*args)


def _attn_kernel(q_ref, k_ref, v_ref, lq1_ref, lk1_ref, lq2_ref, lk2_ref, sw_ref, o_ref,
                 *, tq, tkc, n_kv, n_sub, lambda_init):
    for u in range(n_sub):
        _attn_tile(q_ref, k_ref, v_ref, lq1_ref, lk1_ref, lq2_ref, lk2_ref, sw_ref, o_ref,
                   slice(u * tq, (u + 1) * tq), tq=tq, tkc=tkc, n_kv=n_kv, lambda_init=lambda_init)


def _attn_tile(q_ref, k_ref, v_ref, lq1_ref, lk1_ref, lq2_ref, lk2_ref, sw_ref, o_ref, q_rows,
               *, tq, tkc, n_kv, lambda_init):
    q = q_ref[0, q_rows, :]
    lane = lax.broadcasted_iota(jnp.int32, q.shape, 1)
    zero = jnp.zeros_like(q)
    qs = jnp.concatenate([jnp.where(lane < DA_HEAD_DIM, q, zero),
                          jnp.where(lane >= DA_HEAD_DIM, q, zero)], axis=0)

    def body(j, carry):
        m_prev, l_prev, acc = carry
        rows = pl.ds(pl.multiple_of(j * tkc, tkc), tkc)
        k = k_ref[0, rows, :]
        v = v_ref[0, rows, :]
        s = lax.dot_general(qs, k, NT_DIMS, preferred_element_type=F32)
        m_new = jnp.maximum(m_prev, jnp.max(s, axis=-1, keepdims=True))
        alpha = jnp.exp2(m_prev - m_new)
        p = jnp.exp2(s - m_new)
        l_new = alpha * l_prev + jnp.sum(p, axis=-1, keepdims=True)
        acc = alpha * acc + jnp.dot(p.astype(BF16), v, preferred_element_type=F32)
        return m_new, l_new, acc

    init = (jnp.full((2 * tq, 1), -jnp.inf, F32), jnp.zeros((2 * tq, 1), F32),
            jnp.zeros((2 * tq, 2 * DA_HEAD_DIM), F32))
    _, l_fin, acc = lax.fori_loop(0, n_kv, body, init, unroll=True)

    o = acc / l_fin
    lam = (jnp.exp(jnp.sum(lq1_ref[...] * lk1_ref[...], axis=-1, keepdims=True))
           - jnp.exp(jnp.sum(lq2_ref[...] * lk2_ref[...], axis=-1, keepdims=True)) + lambda_init)
    od = o[0:tq] - lam * o[tq:2 * tq]
    od = _rms(od, sw_ref[...], SUBLN_EPS) * (1.0 - lambda_init)
    o_ref[0, q_rows, :] = od.astype(o_ref.dtype)


def _attn_call(proj, lq1, lk1, lq2, lk2, subln_w, lambda_init):
    b, t, _ = proj.shape
    tq = min(ATT_TQ, t)
    tkc = min(ATT_TK, t)
    n_sub = ATT_SUB if t % (ATT_SUB * tq) == 0 else 1
    assert t % (n_sub * tq) == 0 and t % tkc == 0
    head_w = 2 * DA_HEAD_DIM
    blocks_per_group = GROUP // head_w

    def lam_spec():
        return pl.BlockSpec((1, DA_HEAD_DIM), lambda i, h, qi: (0, 0))

    kern = functools.partial(_attn_kernel, tq=tq, tkc=tkc, n_kv=t // tkc, n_sub=n_sub, lambda_init=lambda_init)
    return pl.pallas_call(
        kern,
        out_shape=jax.ShapeDtypeStruct((b, t, DA_WIDTH), BF16),
        grid=(b, DA_HEADS, t // (n_sub * tq)),
        in_specs=[
            pl.BlockSpec((1, n_sub * tq, head_w), lambda i, h, qi: (i, qi, COL_DQ * blocks_per_group + h)),
            pl.BlockSpec((1, t, head_w), lambda i, h, qi: (i, 0, COL_DK * blocks_per_group + h)),
            pl.BlockSpec((1, t, head_w), lambda i, h, qi: (i, 0, COL_DV * blocks_per_group + h)),
            lam_spec(), lam_spec(), lam_spec(), lam_spec(),
            pl.BlockSpec((1, head_w), lambda i, h, qi: (0, 0)),
        ],
        out_specs=pl.BlockSpec((1, n_sub * tq, head_w), lambda i, h, qi: (i, qi, h)),
        compiler_params=pltpu.CompilerParams(
            dimension_semantics=("parallel", "parallel", "arbitrary"),
            vmem_limit_bytes=VMEM_LIMIT_BYTES),
        name="diff_attn",
    )(proj, proj, proj, lq1.reshape(1, -1), lk1.reshape(1, -1), lq2.reshape(1, -1), lk2.reshape(1, -1),
      subln_w.reshape(1, -1))


def _outffn_kernel(x_ref, ohg_ref, oda_ref, mod_ref, nw_ref, wo_ref, wg_ref, wu_ref, wd_ref, o_ref):
    x = x_ref[0]
    mix = (jnp.dot(ohg_ref[0], wo_ref[0:HG_WIDTH, :], preferred_element_type=F32)
           + jnp.dot(oda_ref[0], wo_ref[HG_WIDTH:HG_WIDTH + DA_WIDTH, :], preferred_element_type=F32))
    x1 = x + mod_ref[0, 2:3, :] * _rms(mix, nw_ref[0:1, :], NORM_EPS)
    h = _rms(x1, nw_ref[1:2, :], NORM_EPS) * (1.0 + mod_ref[0, 4:5, :]) + mod_ref[0, 3:4, :]
    h = h.astype(BF16)
    hidden = wg_ref.shape[1]
    f = None
    for c in range(hidden // FFN_CHUNK):
        cs = slice(c * FFN_CHUNK, (c + 1) * FFN_CHUNK)
        g = jnp.dot(h, wg_ref[:, cs], preferred_element_type=F32)
        u = jnp.dot(h, wu_ref[:, cs], preferred_element_type=F32)
        a = (_silu_exp2(g) * u).astype(BF16)
        fc = jnp.dot(a, wd_ref[cs, :], preferred_element_type=F32)
        f = fc if f is None else f + fc
    o_ref[0] = x1 + mod_ref[0, 5:6, :] * _rms(f, nw_ref[2:3, :], NORM_EPS)


def _outffn_call(x, o_hg, o_da, mod, norms, wo, wg, wu, wd):
    b, t, d = x.shape
    hidden = wg.shape[1]
    tm = min(TM_FFN, t)
    assert t % tm == 0 and hidden % FFN_CHUNK == 0

    def resident(shape):
        return pl.BlockSpec(shape, lambda i, j: (0, 0), pipeline_mode=pl.Buffered(1))

    return pl.pallas_call(
        _outffn_kernel,
        out_shape=jax.ShapeDtypeStruct((b, t, d), F32),
        grid=(b, t // tm),
        in_specs=[
            pl.BlockSpec((1, tm, d), lambda i, j: (i, j, 0)),
            pl.BlockSpec((1, tm, HG_WIDTH), lambda i, j: (i, j, 0)),
            pl.BlockSpec((1, tm, DA_WIDTH), lambda i, j: (i, j, 0)),
            pl.BlockSpec((1, N_MOD, d), lambda i, j: (i, 0, 0)),
            pl.BlockSpec((3, d), lambda i, j: (0, 0)),
            resident((HG_WIDTH + DA_WIDTH, d)),
            resident((d, hidden)),
            resident((d, hidden)),
            resident((hidden, d)),
        ],
        out_specs=pl.BlockSpec((1, tm, d), lambda i, j: (i, j, 0)),
        compiler_params=pltpu.CompilerParams(
            dimension_semantics=("parallel", "parallel"), vmem_limit_bytes=VMEM_LIMIT_BYTES),
        name="outproj_ffn",
    )(x, o_hg, o_da, mod, norms, wo, wg, wu, wd)


def _rotary_tables(t):
    dh = DA_HEAD_DIM
    inv = 1.0 / (ROPE_THETA ** (jnp.arange(0, dh, 2, dtype=F32) / dh))
    ang = jnp.arange(t, dtype=F32)[:, None] * inv[None, :]
    cos = jnp.cos(ang)
    sin = jnp.sin(ang)
    reps = LANES // dh
    cos_t = jnp.tile(jnp.concatenate([cos, cos], axis=-1), (1, reps))
    sin_t = jnp.tile(jnp.concatenate([-sin, sin], axis=-1), (1, reps))
    return cos_t, sin_t


def kernel(x_prompt, x_sample, c_prompt, c_sample, w_ada, b_ada, norm_pre_mix, norm_post_mix, norm_pre_ffn, norm_post_ffn, w_in, hg_lower_bounds, hg_gnorm, da_lambda_q1, da_lambda_k1, da_lambda_q2, da_lambda_k2, da_subln, w_out, w_ffn_gate, w_ffn_up, w_ffn_down):
    depth = w_in.shape[0]
    d = x_prompt.shape[-1]

    lb_soft = jax.nn.softmax(hg_lower_bounds.astype(F32), axis=1)
    lb_all = jnp.cumsum(lb_soft, axis=1) - lb_soft[:, :1]

    c_all = jnp.concatenate([c_prompt, c_sample], axis=0)
    mod_all = _ada_call(c_all, w_ada, b_ada)

    w_in_b = w_in.astype(BF16)
    w_out_b = w_out.astype(BF16)
    wg_b = w_ffn_gate.astype(BF16)
    wu_b = w_ffn_up.astype(BF16)
    wd_b = w_ffn_down.astype(BF16)

    def trunk(x, row0):
        b, t, _ = x.shape
        cos, sin = _rotary_tables(t)
        for l in range(depth):
            mod = mod_all[l, row0:row0 + b].reshape(b, N_MOD, d)
            lambda_init = 0.8 - 0.6 * math.exp(-0.3 * l)
            proj = _inproj_call(x, mod, norm_pre_mix[l], w_in_b[l], cos, sin)
            o_f = _hgrn_call(proj, lb_all[0, l], None, None, reverse=False)
            o_hg = _hgrn_call(proj, lb_all[1, l], hg_gnorm[l], o_f, reverse=True)
            o_da = _attn_call(proj, da_lambda_q1[l], da_lambda_k1[l], da_lambda_q2[l], da_lambda_k2[l],
                              da_subln[l], lambda_init)
            norms = jnp.stack([norm_post_mix[l], norm_pre_ffn[l], norm_post_ffn[l]], axis=0)
            x = _outffn_call(x, o_hg, o_da, mod, norms, w_out_b[l], wg_b[l], wu_b[l], wd_b[l])
        return x

    y_prompt = trunk(x_prompt, 0)
    y_sample = trunk(x_sample, c_prompt.shape[0])
    return (y_prompt, y_sample)
```

```python
import functools
import math

import numpy as np
import jax
import jax.numpy as jnp
from jax import lax
from jax.experimental import pallas as pl
from jax.experimental.pallas import tpu as pltpu

F32 = jnp.float32
BF16 = jnp.bfloat16

HG_WIDTH = 512
HG_HEADS = 4
HG_DK = 128
DA_WIDTH = 512
DA_HEADS = 4
DA_HEAD_DIM = 64
ROPE_THETA = 10000.0
NORM_EPS = 1e-6
SUBLN_EPS = 1e-5
LOG_FLOOR = 1e-30
LOG2_E = math.log2(math.e)
N_MOD = 6

LANES = 128
SUBLANES = 8
VMEM_LIMIT_BYTES = 56 * 1024 * 1024

COL_HQ, COL_HFF, COL_HFB, COL_HI, COL_HG, COL_DQ, COL_DK, COL_DV = range(8)
GROUP = 512

ADA_TN = 1536
TM_PROJ = 1024
TM_FFN = 512
FFN_CHUNK = 256
HG_CHUNK = 128
HG_TILE = 512
HG_SUB = SUBLANES
HG_ROWS = 4
HG_UNROLL = 2
ATT_TQ = 512
ATT_TK = 2048
ATT_SUB = 2

NT_DIMS = (((1,), (1,)), ((), ()))
TN_DIMS = (((0,), (0,)), ((), ()))


def _silu(x):
    return x * jax.nn.sigmoid(x)


def _rms(x, w, eps):
    ms = jnp.mean(x * x, axis=-1, keepdims=True)
    return x * lax.rsqrt(ms + eps) * w


def _ada_kernel(c_ref, w_ref, b_ref, o_ref):
    s = _silu(c_ref[...]).astype(BF16)
    w = w_ref[0].astype(BF16)
    o_ref[0] = jnp.dot(s, w, preferred_element_type=F32) + b_ref[0]


def _ada_call(c_all, w_ada, b_ada):
    depth, d, n = w_ada.shape
    rows = c_all.shape[0]
    tn = ADA_TN
    assert n % tn == 0
    return pl.pallas_call(
        _ada_kernel,
        out_shape=jax.ShapeDtypeStruct((depth, rows, n), F32),
        grid=(depth, n // tn),
        in_specs=[
            pl.BlockSpec((rows, d), lambda l, j: (0, 0)),
            pl.BlockSpec((1, d, tn), lambda l, j: (l, 0, j)),
            pl.BlockSpec((1, 1, tn), lambda l, j: (l, 0, j)),
        ],
        out_specs=pl.BlockSpec((1, rows, tn), lambda l, j: (l, 0, j)),
        compiler_params=pltpu.CompilerParams(
            dimension_semantics=("parallel", "parallel"), vmem_limit_bytes=VMEM_LIMIT_BYTES),
        name="adaln_mod",
    )(c_all, w_ada, b_ada.reshape(depth, 1, n))


def _inproj_kernel(x_ref, mod_ref, nw_ref, w_ref, cos_ref, sin_ref, o_ref):
    x = x_ref[0]
    h = _rms(x, nw_ref[...], NORM_EPS) * (1.0 + mod_ref[0, 1:2, :]) + mod_ref[0, 0:1, :]
    h = h.astype(BF16)
    cos = cos_ref[...]
    sin = sin_ref[...]
    lane = lax.broadcasted_iota(jnp.int32, cos.shape, 1)
    first_half = (lane % DA_HEAD_DIM) < (DA_HEAD_DIM // 2)
    n_groups = w_ref.shape[1] // GROUP
    for n in range(n_groups):
        y = jnp.dot(h, w_ref[:, n * GROUP:(n + 1) * GROUP], preferred_element_type=F32)
        if n in (COL_DQ, COL_DK):
            pieces = []
            for j in range(GROUP // LANES):
                yj = y[:, j * LANES:(j + 1) * LANES]
                partner = jnp.where(first_half,
                                    pltpu.roll(yj, LANES - DA_HEAD_DIM // 2, 1),
                                    pltpu.roll(yj, DA_HEAD_DIM // 2, 1))
                r = yj * cos + partner * sin
                if n == COL_DQ:
                    r = r * (DA_HEAD_DIM ** -0.5 * LOG2_E)
                pieces.append(r)
            y = jnp.concatenate(pieces, axis=1)
        o_ref[0, :, n * GROUP:(n + 1) * GROUP] = y.astype(BF16)


def _inproj_call(x, mod, nw, w_bf16, cos, sin):
    b, t, d = x.shape
    n = w_bf16.shape[1]
    tm = min(TM_PROJ, t)
    assert t % tm == 0
    return pl.pallas_call(
        _inproj_kernel,
        out_shape=jax.ShapeDtypeStruct((b, t, n), BF16),
        grid=(b, t // tm),
        in_specs=[
            pl.BlockSpec((1, tm, d), lambda i, j: (i, j, 0)),
            pl.BlockSpec((1, N_MOD, d), lambda i, j: (i, 0, 0)),
            pl.BlockSpec((1, d), lambda i, j: (0, 0)),
            pl.BlockSpec((d, n), lambda i, j: (0, 0), pipeline_mode=pl.Buffered(1)),
            pl.BlockSpec((tm, LANES), lambda i, j: (j, 0)),
            pl.BlockSpec((tm, LANES), lambda i, j: (j, 0)),
        ],
        out_specs=pl.BlockSpec((1, tm, n), lambda i, j: (i, j, 0)),
        compiler_params=pltpu.CompilerParams(
            dimension_semantics=("parallel", "parallel"), vmem_limit_bytes=VMEM_LIMIT_BYTES),
        name="norm_inproj",
    )(x, mod, nw.reshape(1, d), w_bf16, cos, sin)


def _hgrn_levels(ch):
    levels = []
    m = HG_SUB
    while m < ch:
        levels.append(m)
        m *= 2
    return levels


@functools.lru_cache(maxsize=None)
def _hgrn_constants(ch, reverse):
    idx = np.arange(ch)
    nsb = ch // HG_SUB
    order = (idx % nsb) * HG_SUB + idx // nsb
    if reverse:
        cum = (idx[None, :] >= idx[:, None]).astype(np.float32)
    else:
        cum = (idx[None, :] <= idx[:, None]).astype(np.float32)
    refs = []
    masks = []
    for m in _hgrn_levels(ch):
        pair = idx // (2 * m)
        in_second = (idx % (2 * m)) >= m
        ref = pair * 2 * m + (m if reverse else m - 1)
        refs.append(cum[ref[::HG_SUB]])
        later = ~in_second if reverse else in_second
        earlier = ~later
        masks.append(((pair[:, None] == pair[None, :]) & later[:, None] & earlier[None, :]).astype(np.float32))
    sel = np.ix_(order, order)
    w = np.concatenate([cum[sel]] + [rf[:, order] for rf in refs], axis=0)
    w = np.concatenate([w, w], axis=1)
    masks = np.stack([mk[sel] for mk in masks], axis=0)
    perm = np.eye(ch, dtype=np.float32)[order]
    return perm, w, masks


def _silu_exp2(x):
    return x / (1.0 + jnp.exp2(-LOG2_E * x))


def _hgrn_kernel(*refs, reverse, final, ch, n_chunks):
    if final:
        (hq_ref, hf_ref, hi_ref, hg_ref, of_ref, lb_ref, gw_ref, perm_ref, permt_ref, w_ref, mask_ref,
         o_ref, st_ref) = refs
    else:
        (hq_ref, hf_ref, hi_ref, lb_ref, perm_ref, w_ref, mask_ref, o_ref, st_ref) = refs
    n_lev = len(_hgrn_levels(ch))
    nsb = ch // HG_SUB

    @pl.when(pl.program_id(1) == 0)
    def _():
        st_ref[...] = jnp.zeros_like(st_ref)

    lb = lb_ref[...]
    lb_floor = jnp.maximum(lb, LOG_FLOOR)
    one_m_lb = 1.0 - lb
    ones = jnp.ones((LANES, LANES), BF16)

    def slab(x, t):
        return x[t * nsb:(t + 1) * nsb]

    nb = hq_ref.shape[0]
    n_groups = 4 if final else 3
    units = [(bi, h) for bi in range(nb) for h in range(HG_HEADS)]
    pairs = [(t, s) for t in range(HG_SUB) for s in range(HG_SUB) if ((s >= t) if reverse else (s <= t))]

    def chunk_body(i, carry):
        c = (n_chunks - 1 - i) if reverse else i
        rows = pl.ds(pl.multiple_of(c * ch, ch), ch)
        parts = []
        for bi in range(nb):
            parts += [hq_ref[bi, rows, :], hf_ref[bi, rows, :], hi_ref[bi, rows, :]]
            if final:
                parts.append(hg_ref[bi, rows, :])
        xp = jnp.dot(perm_ref[...], jnp.concatenate(parts, axis=1), preferred_element_type=F32)

        def col(group, bi, h):
            c0 = (bi * n_groups + group) * HG_WIDTH + h * HG_DK
            return xp[:, c0:c0 + HG_DK]

        q, kk, qb, kb, v, vb, g_hi, g_lo = [], [], [], [], [], [], [], []
        for bi, h in units:
            cols = slice(h * HG_DK, (h + 1) * HG_DK)
            z = col(1, bi, h)
            e = jnp.exp2(-LOG2_E * jnp.abs(z))
            r = 1.0 / (1.0 + e)
            er = e * r
            pos = z >= 0.0
            g = jnp.log2(lb_floor[:, cols] + one_m_lb[:, cols] * jnp.where(pos, r, er))
            kk.append(one_m_lb[:, cols] * jnp.where(pos, er, r))
            q.append(_silu_exp2(col(0, bi, h)))
            qb.append(q[-1].astype(BF16))
            kb.append(kk[-1].astype(BF16))
            v.append(col(2, bi, h))
            vb.append(col(2, bi, h).astype(BF16))
            g_hi.append(g.astype(BF16))
            g_lo.append((g - g_hi[-1].astype(F32)).astype(BF16))

        g2 = jnp.concatenate([jnp.concatenate(g_hi, axis=1), jnp.concatenate(g_lo, axis=1)], axis=0)
        y_all = jnp.dot(w_ref[...], g2, preferred_element_type=F32)
        y = [y_all[:, u * HG_DK:(u + 1) * HG_DK] for u in range(len(units))]
        b = [yu[0:ch] for yu in y]
        b_end = [bu[0:1] if reverse else bu[ch - 1:ch] for bu in b]

        a = []
        for u in range(len(units)):
            au = None
            for l in range(n_lev):
                ref = y[u][ch + l * nsb:ch + (l + 1) * nsb]
                el = jnp.exp2(-jnp.abs(b[u] - jnp.concatenate([ref] * HG_SUB, axis=0))).astype(BF16)
                al = lax.dot_general(qb[u] * el, kb[u] * el, NT_DIMS,
                                     preferred_element_type=F32) * mask_ref[l]
                au = al if au is None else au + al
            a.append(au.astype(BF16))

        pieces = []
        for u in range(len(units)):
            for t, s in pairs:
                p = slab(qb[u], t) * slab(kb[u], s)
                if s != t:
                    p = p * jnp.exp2(slab(b[u], t) - slab(b[u], s)).astype(BF16)
                pieces.append(p)
        rsum = jnp.dot(jnp.concatenate(pieces, axis=0), ones, preferred_element_type=F32)

        outs = []
        for u, (bi, h) in enumerate(units):
            cols = slice(h * HG_DK, (h + 1) * HG_DK)
            od = [None] * HG_SUB
            for n, (t, s) in enumerate(pairs):
                n0 = (u * len(pairs) + n) * nsb
                term = rsum[n0:n0 + nsb] * slab(v[u], s)
                od[t] = term if od[t] is None else od[t] + term
            qs = (q[u] * jnp.exp2(b[u])).astype(BF16)
            ks = (kk[u] * jnp.exp2(b_end[u] - b[u])).astype(BF16)
            st = st_ref[bi, h]
            o = (jnp.dot(a[u], vb[u], preferred_element_type=F32) + jnp.concatenate(od, axis=0)
                 + lax.dot_general(qs, st.astype(BF16), NT_DIMS, preferred_element_type=F32))
            st_ref[bi, h] = (st * jnp.exp2(b_end[u])
                             + lax.dot_general(vb[u], ks, TN_DIMS, preferred_element_type=F32))
            if final:
                o = o + of_ref[bi, rows, cols]
                o = _rms(o, gw_ref[...], NORM_EPS) * _silu_exp2(col(3, bi, h))
                outs.append(o.astype(BF16))
            else:
                o_ref[bi, rows, cols] = o
        if final:
            y_out = jnp.dot(permt_ref[...], jnp.concatenate(outs, axis=1), preferred_element_type=F32)
            for bi in range(nb):
                o_ref[bi, rows, :] = y_out[:, bi * HG_WIDTH:(bi + 1) * HG_WIDTH].astype(o_ref.dtype)
        return carry

    lax.fori_loop(0, n_chunks, chunk_body, 0, unroll=HG_UNROLL)


def _hgrn_call(proj, lb_dir, gnorm_w, o_fwd, *, reverse):
    b, t, _ = proj.shape
    final = o_fwd is not None
    ch = HG_CHUNK
    tile = min(HG_TILE, t)
    assert t % tile == 0 and tile % ch == 0
    nt = t // tile
    perm_np, w_np, mask_np = _hgrn_constants(ch, reverse)

    def tok(j):
        return (nt - 1 - j) if reverse else j

    nb = HG_ROWS if b % HG_ROWS == 0 else 1

    def col_spec(group):
        return pl.BlockSpec((nb, tile, GROUP), lambda i, j: (i, tok(j), group))

    def const_spec(shape):
        return pl.BlockSpec(shape, lambda i, j: (0,) * len(shape))

    in_specs = [col_spec(COL_HQ), col_spec(COL_HFB if reverse else COL_HFF), col_spec(COL_HI)]
    args = [proj, proj, proj]
    if final:
        in_specs += [col_spec(COL_HG), pl.BlockSpec((nb, tile, HG_WIDTH), lambda i, j: (i, tok(j), 0))]
        args += [proj, o_fwd]
    in_specs.append(const_spec((1, HG_WIDTH)))
    args.append(lb_dir.reshape(1, HG_WIDTH))
    if final:
        in_specs.append(const_spec((1, HG_DK)))
        args.append(gnorm_w.reshape(1, HG_DK))
    in_specs.append(const_spec(perm_np.shape))
    args.append(jnp.asarray(perm_np, BF16))
    if final:
        in_specs.append(const_spec(perm_np.shape))
        args.append(jnp.asarray(perm_np.T, BF16))
    in_specs += [const_spec(w_np.shape), const_spec(mask_np.shape)]
    args += [jnp.asarray(w_np, BF16), jnp.asarray(mask_np, F32)]

    kern = functools.partial(_hgrn_kernel, reverse=reverse, final=final, ch=ch, n_chunks=tile // ch)
    return pl.pallas_call(
        kern,
        out_shape=jax.ShapeDtypeStruct((b, t, HG_WIDTH), BF16 if final else F32),
        grid=(b // nb, nt),
        in_specs=in_specs,
        out_specs=pl.BlockSpec((nb, tile, HG_WIDTH), lambda i, j: (i, tok(j), 0)),
        scratch_shapes=[pltpu.VMEM((nb, HG_HEADS, HG_DK, HG_DK), F32)],
        compiler_params=pltpu.CompilerParams(
            dimension_semantics=("parallel", "arbitrary"), vmem_limit_bytes=VMEM_LIMIT_BYTES),
        name="hgrn2_bwd" if reverse else "hgrn2_fwd",
    )(*args)


def _attn_kernel(q_ref, k_ref, v_ref, lq1_ref, lk1_ref, lq2_ref, lk2_ref, sw_ref, o_ref,
                 *, tq, tkc, n_kv, n_sub, lambda_init):
    for u in range(n_sub):
        _attn_tile(q_ref, k_ref, v_ref, lq1_ref, lk1_ref, lq2_ref, lk2_ref, sw_ref, o_ref,
                   slice(u * tq, (u + 1) * tq), tq=tq, tkc=tkc, n_kv=n_kv, lambda_init=lambda_init)


def _attn_tile(q_ref, k_ref, v_ref, lq1_ref, lk1_ref, lq2_ref, lk2_ref, sw_ref, o_ref, q_rows,
               *, tq, tkc, n_kv, lambda_init):
    q = q_ref[0, q_rows, :]
    lane = lax.broadcasted_iota(jnp.int32, q.shape, 1)
    zero = jnp.zeros_like(q)
    qs = jnp.concatenate([jnp.where(lane < DA_HEAD_DIM, q, zero),
                          jnp.where(lane >= DA_HEAD_DIM, q, zero)], axis=0)

    def body(j, carry):
        m_prev, l_prev, acc = carry
        rows = pl.ds(pl.multiple_of(j * tkc, tkc), tkc)
        k = k_ref[0, rows, :]
        v = v_ref[0, rows, :]
        s = lax.dot_general(qs, k, NT_DIMS, preferred_element_type=F32)
        m_new = jnp.maximum(m_prev, jnp.max(s, axis=-1, keepdims=True))
        alpha = jnp.exp2(m_prev - m_new)
        p = jnp.exp2(s - m_new)
        l_new = alpha * l_prev + jnp.sum(p, axis=-1, keepdims=True)
        acc = alpha * acc + jnp.dot(p.astype(BF16), v, preferred_element_type=F32)
        return m_new, l_new, acc

    init = (jnp.full((2 * tq, 1), -jnp.inf, F32), jnp.zeros((2 * tq, 1), F32),
            jnp.zeros((2 * tq, 2 * DA_HEAD_DIM), F32))
    _, l_fin, acc = lax.fori_loop(0, n_kv, body, init, unroll=True)

    o = acc / l_fin
    lam = (jnp.exp(jnp.sum(lq1_ref[...] * lk1_ref[...], axis=-1, keepdims=True))
           - jnp.exp(jnp.sum(lq2_ref[...] * lk2_ref[...], axis=-1, keepdims=True)) + lambda_init)
    od = o[0:tq] - lam * o[tq:2 * tq]
    od = _rms(od, sw_ref[...], SUBLN_EPS) * (1.0 - lambda_init)
    o_ref[0, q_rows, :] = od.astype(o_ref.dtype)


def _attn_call(proj, lq1, lk1, lq2, lk2, subln_w, lambda_init):
    b, t, _ = proj.shape
    tq = min(ATT_TQ, t)
    tkc = min(ATT_TK, t)
    n_sub = ATT_SUB if t % (ATT_SUB * tq) == 0 else 1
    assert t % (n_sub * tq) == 0 and t % tkc == 0
    head_w = 2 * DA_HEAD_DIM
    blocks_per_group = GROUP // head_w

    def lam_spec():
        return pl.BlockSpec((1, DA_HEAD_DIM), lambda i, h, qi: (0, 0))

    kern = functools.partial(_attn_kernel, tq=tq, tkc=tkc, n_kv=t // tkc, n_sub=n_sub, lambda_init=lambda_init)
    return pl.pallas_call(
        kern,
        out_shape=jax.ShapeDtypeStruct((b, t, DA_WIDTH), BF16),
        grid=(b, DA_HEADS, t // (n_sub * tq)),
        in_specs=[
            pl.BlockSpec((1, n_sub * tq, head_w), lambda i, h, qi: (i, qi, COL_DQ * blocks_per_group + h)),
            pl.BlockSpec((1, t, head_w), lambda i, h, qi: (i, 0, COL_DK * blocks_per_group + h)),
            pl.BlockSpec((1, t, head_w), lambda i, h, qi: (i, 0, COL_DV * blocks_per_group + h)),
            lam_spec(), lam_spec(), lam_spec(), lam_spec(),
            pl.BlockSpec((1, head_w), lambda i, h, qi: (0, 0)),
        ],
        out_specs=pl.BlockSpec((1, n_sub * tq, head_w), lambda i, h, qi: (i, qi, h)),
        compiler_params=pltpu.CompilerParams(
            dimension_semantics=("parallel", "parallel", "arbitrary"),
            vmem_limit_bytes=VMEM_LIMIT_BYTES),
        name="diff_attn",
    )(proj, proj, proj, lq1.reshape(1, -1), lk1.reshape(1, -1), lq2.reshape(1, -1), lk2.reshape(1, -1),
      subln_w.reshape(1, -1))


def _outffn_kernel(x_ref, ohg_ref, oda_ref, mod_ref, nw_ref, wo_ref, wg_ref, wu_ref, wd_ref, o_ref):
    x = x_ref[0]
    mix = (jnp.dot(ohg_ref[0], wo_ref[0:HG_WIDTH, :], preferred_element_type=F32)
           + jnp.dot(oda_ref[0], wo_ref[HG_WIDTH:HG_WIDTH + DA_WIDTH, :], preferred_element_type=F32))
    x1 = x + mod_ref[0, 2:3, :] * _rms(mix, nw_ref[0:1, :], NORM_EPS)
    h = _rms(x1, nw_ref[1:2, :], NORM_EPS) * (1.0 + mod_ref[0, 4:5, :]) + mod_ref[0, 3:4, :]
    h = h.astype(BF16)
    hidden = wg_ref.shape[1]
    f = None
    for c in range(hidden // FFN_CHUNK):
        cs = slice(c * FFN_CHUNK, (c + 1) * FFN_CHUNK)
        g = jnp.dot(h, wg_ref[:, cs], preferred_element_type=F32)
        u = jnp.dot(h, wu_ref[:, cs], preferred_element_type=F32)
        a = (_silu_exp2(g) * u).astype(BF16)
        fc = jnp.dot(a, wd_ref[cs, :], preferred_element_type=F32)
        f = fc if f is None else f + fc
    o_ref[0] = x1 + mod_ref[0, 5:6, :] * _rms(f, nw_ref[2:3, :], NORM_EPS)


def _outffn_call(x, o_hg, o_da, mod, norms, wo, wg, wu, wd):
    b, t, d = x.shape
    hidden = wg.shape[1]
    tm = min(TM_FFN, t)
    assert t % tm == 0 and hidden % FFN_CHUNK == 0

    def resident(shape):
        return pl.BlockSpec(shape, lambda i, j: (0, 0), pipeline_mode=pl.Buffered(1))

    return pl.pallas_call(
        _outffn_kernel,
        out_shape=jax.ShapeDtypeStruct((b, t, d), F32),
        grid=(b, t // tm),
        in_specs=[
            pl.BlockSpec((1, tm, d), lambda i, j: (i, j, 0)),
            pl.BlockSpec((1, tm, HG_WIDTH), lambda i, j: (i, j, 0)),
            pl.BlockSpec((1, tm, DA_WIDTH), lambda i, j: (i, j, 0)),
            pl.BlockSpec((1, N_MOD, d), lambda i, j: (i, 0, 0)),
            pl.BlockSpec((3, d), lambda i, j: (0, 0)),
            resident((HG_WIDTH + DA_WIDTH, d)),
            resident((d, hidden)),
            resident((d, hidden)),
            resident((hidden, d)),
        ],
        out_specs=pl.BlockSpec((1, tm, d), lambda i, j: (i, j, 0)),
        compiler_params=pltpu.CompilerParams(
            dimension_semantics=("parallel", "parallel"), vmem_limit_bytes=VMEM_LIMIT_BYTES),
        name="outproj_ffn",
    )(x, o_hg, o_da, mod, norms, wo, wg, wu, wd)


def _rotary_tables(t):
    dh = DA_HEAD_DIM
    inv = 1.0 / (ROPE_THETA ** (jnp.arange(0, dh, 2, dtype=F32) / dh))
    ang = jnp.arange(t, dtype=F32)[:, None] * inv[None, :]
    cos = jnp.cos(ang)
    sin = jnp.sin(ang)
    reps = LANES // dh
    cos_t = jnp.tile(jnp.concatenate([cos, cos], axis=-1), (1, reps))
    sin_t = jnp.tile(jnp.concatenate([-sin, sin], axis=-1), (1, reps))
    return cos_t, sin_t


def kernel(x_prompt, x_sample, c_prompt, c_sample, w_ada, b_ada, norm_pre_mix, norm_post_mix, norm_pre_ffn, norm_post_ffn, w_in, hg_lower_bounds, hg_gnorm, da_lambda_q1, da_lambda_k1, da_lambda_q2, da_lambda_k2, da_subln, w_out, w_ffn_gate, w_ffn_up, w_ffn_down):
    depth = w_in.shape[0]
    d = x_prompt.shape[-1]

    lb_soft = jax.nn.softmax(hg_lower_bounds.astype(F32), axis=1)
    lb_all = jnp.cumsum(lb_soft, axis=1) - lb_soft[:, :1]

    c_all = jnp.concatenate([c_prompt, c_sample], axis=0)
    mod_all = _ada_call(c_all, w_ada, b_ada)

    w_in_b = w_in.astype(BF16)
    w_out_b = w_out.astype(BF16)
    wg_b = w_ffn_gate.astype(BF16)
    wu_b = w_ffn_up.astype(BF16)
    wd_b = w_ffn_down.astype(BF16)

    def trunk(x, row0):
        b, t, _ = x.shape
        cos, sin = _rotary_tables(t)
        for l in range(depth):
            mod = mod_all[l, row0:row0 + b].reshape(b, N_MOD, d)
            lambda_init = 0.8 - 0.6 * math.exp(-0.3 * l)
            proj = _inproj_call(x, mod, norm_pre_mix[l], w_in_b[l], cos, sin)
            o_f = _hgrn_call(proj, lb_all[0, l], None, None, reverse=False)
            o_hg = _hgrn_call(proj, lb_all[1, l], hg_gnorm[l], o_f, reverse=True)
            o_da = _attn_call(proj, da_lambda_q1[l], da_lambda_k1[l], da_lambda_q2[l], da_lambda_k2[l],
                              da_subln[l], lambda_init)
            norms = jnp.stack([norm_post_mix[l], norm_pre_ffn[l], norm_post_ffn[l]], axis=0)
            x = _outffn_call(x, o_hg, o_da, mod, norms, w_out_b[l], wg_b[l], wu_b[l], wd_b[l])
        return x

    y_prompt = trunk(x_prompt, 0)
    y_sample = trunk(x_sample, c_prompt.shape[0])
    return (y_prompt, y_sample)
```

```python
import functools
import math

import numpy as np
import jax
import jax.numpy as jnp
from jax import lax
from jax.experimental import pallas as pl
from jax.experimental.pallas import tpu as pltpu

F32 = jnp.float32
BF16 = jnp.bfloat16

HG_WIDTH = 512
HG_HEADS = 4
HG_DK = 128
DA_WIDTH = 512
DA_HEADS = 4
DA_HEAD_DIM = 64
ROPE_THETA = 10000.0
NORM_EPS = 1e-6
SUBLN_EPS = 1e-5
LOG_FLOOR = 1e-30
LOG2_E = math.log2(math.e)
N_MOD = 6

LANES = 128
SUBLANES = 8
VMEM_LIMIT_BYTES = 56 * 1024 * 1024

COL_HQ, COL_HFF, COL_HFB, COL_HI, COL_HG, COL_DQ, COL_DK, COL_DV = range(8)
GROUP = 512

ADA_TN = 1536
TM_PROJ = 1024
TM_FFN = 512
FFN_PRO_PARTS = 4
FFN_CHUNK = 256
HG_CHUNK = 128
HG_TILE = 512
HG_SUB = SUBLANES
HG_ROWS = 4
HG_UNROLL = 2
ATT_TQ = 512
ATT_TK = 2048
ATT_SUB = 2

NT_DIMS = (((1,), (1,)), ((), ()))
TN_DIMS = (((0,), (0,)), ((), ()))


def _silu(x):
    return x * jax.nn.sigmoid(x)


def _rms(x, w, eps):
    ms = jnp.mean(x * x, axis=-1, keepdims=True)
    return x * lax.rsqrt(ms + eps) * w


def _ada_kernel(c_ref, w_ref, b_ref, o_ref):
    s = _silu(c_ref[...]).astype(BF16)
    w = w_ref[0].astype(BF16)
    o_ref[0] = jnp.dot(s, w, preferred_element_type=F32) + b_ref[0]


def _ada_call(c_all, w_ada, b_ada):
    depth, d, n = w_ada.shape
    rows = c_all.shape[0]
    tn = ADA_TN
    assert n % tn == 0
    return pl.pallas_call(
        _ada_kernel,
        out_shape=jax.ShapeDtypeStruct((depth, rows, n), F32),
        grid=(depth, n // tn),
        in_specs=[
            pl.BlockSpec((rows, d), lambda l, j: (0, 0)),
            pl.BlockSpec((1, d, tn), lambda l, j: (l, 0, j)),
            pl.BlockSpec((1, 1, tn), lambda l, j: (l, 0, j)),
        ],
        out_specs=pl.BlockSpec((1, rows, tn), lambda l, j: (l, 0, j)),
        compiler_params=pltpu.CompilerParams(
            dimension_semantics=("parallel", "parallel"), vmem_limit_bytes=VMEM_LIMIT_BYTES),
        name="adaln_mod",
    )(c_all, w_ada, b_ada.reshape(depth, 1, n))


def _inproj_kernel(x_ref, mod_ref, nw_ref, w_ref, cos_ref, sin_ref, o_ref):
    x = x_ref[0]
    h = _rms(x, nw_ref[...], NORM_EPS) * (1.0 + mod_ref[0, 1:2, :]) + mod_ref[0, 0:1, :]
    h = h.astype(BF16)
    cos = cos_ref[...]
    sin = sin_ref[...]
    lane = lax.broadcasted_iota(jnp.int32, cos.shape, 1)
    first_half = (lane % DA_HEAD_DIM) < (DA_HEAD_DIM // 2)
    n_groups = w_ref.shape[1] // GROUP
    for n in range(n_groups):
        y = jnp.dot(h, w_ref[:, n * GROUP:(n + 1) * GROUP], preferred_element_type=F32)
        if n in (COL_DQ, COL_DK):
            pieces = []
            for j in range(GROUP // LANES):
                yj = y[:, j * LANES:(j + 1) * LANES]
                partner = jnp.where(first_half,
                                    pltpu.roll(yj, LANES - DA_HEAD_DIM // 2, 1),
                                    pltpu.roll(yj, DA_HEAD_DIM // 2, 1))
                r = yj * cos + partner * sin
                if n == COL_DQ:
                    r = r * (DA_HEAD_DIM ** -0.5 * LOG2_E)
                pieces.append(r)
            y = jnp.concatenate(pieces, axis=1)
        o_ref[0, :, n * GROUP:(n + 1) * GROUP] = y.astype(BF16)


def _inproj_call(x, mod, nw, w_bf16, cos, sin):
    b, t, d = x.shape
    n = w_bf16.shape[1]
    tm = min(TM_PROJ, t)
    assert t % tm == 0
    return pl.pallas_call(
        _inproj_kernel,
        out_shape=jax.ShapeDtypeStruct((b, t, n), BF16),
        grid=(b, t // tm),
        in_specs=[
            pl.BlockSpec((1, tm, d), lambda i, j: (i, j, 0)),
            pl.BlockSpec((1, N_MOD, d), lambda i, j: (i, 0, 0)),
            pl.BlockSpec((1, d), lambda i, j: (0, 0)),
            pl.BlockSpec((d, n), lambda i, j: (0, 0), pipeline_mode=pl.Buffered(1)),
            pl.BlockSpec((tm, LANES), lambda i, j: (j, 0)),
            pl.BlockSpec((tm, LANES), lambda i, j: (j, 0)),
        ],
        out_specs=pl.BlockSpec((1, tm, n), lambda i, j: (i, j, 0)),
        compiler_params=pltpu.CompilerParams(
            dimension_semantics=("parallel", "parallel"), vmem_limit_bytes=VMEM_LIMIT_BYTES),
        name="norm_inproj",
    )(x, mod, nw.reshape(1, d), w_bf16, cos, sin)


def _hgrn_levels(ch):
    levels = []
    m = HG_SUB
    while m < ch:
        levels.append(m)
        m *= 2
    return levels


@functools.lru_cache(maxsize=None)
def _hgrn_constants(ch, reverse):
    idx = np.arange(ch)
    nsb = ch // HG_SUB
    order = (idx % nsb) * HG_SUB + idx // nsb
    if reverse:
        cum = (idx[None, :] >= idx[:, None]).astype(np.float32)
    else:
        cum = (idx[None, :] <= idx[:, None]).astype(np.float32)
    refs = []
    masks = []
    for m in _hgrn_levels(ch):
        pair = idx // (2 * m)
        in_second = (idx % (2 * m)) >= m
        ref = pair * 2 * m + (m if reverse else m - 1)
        refs.append(cum[ref[::HG_SUB]])
        later = ~in_second if reverse else in_second
        earlier = ~later
        masks.append(((pair[:, None] == pair[None, :]) & later[:, None] & earlier[None, :]).astype(np.float32))
    sel = np.ix_(order, order)
    w = np.concatenate([cum[sel]] + [rf[:, order] for rf in refs], axis=0)
    w = np.concatenate([w, w], axis=1)
    masks = np.stack([mk[sel] for mk in masks], axis=0)
    perm = np.eye(ch, dtype=np.float32)[order]
    return perm, w, masks


def _silu_exp2(x):
    return x / (1.0 + jnp.exp2(-LOG2_E * x))


def _hgrn_kernel(*refs, reverse, final, ch, n_chunks):
    if final:
        (hq_ref, hf_ref, hi_ref, hg_ref, of_ref, lb_ref, gw_ref, perm_ref, permt_ref, w_ref, mask_ref,
         o_ref, st_ref) = refs
    else:
        (hq_ref, hf_ref, hi_ref, lb_ref, perm_ref, w_ref, mask_ref, o_ref, st_ref) = refs
    n_lev = len(_hgrn_levels(ch))
    nsb = ch // HG_SUB

    @pl.when(pl.program_id(1) == 0)
    def _():
        st_ref[...] = jnp.zeros_like(st_ref)

    lb = lb_ref[...]
    lb_floor = jnp.maximum(lb, LOG_FLOOR)
    one_m_lb = 1.0 - lb
    ones = jnp.ones((LANES, LANES), BF16)

    def slab(x, t):
        return x[t * nsb:(t + 1) * nsb]

    nb = hq_ref.shape[0]
    n_groups = 4 if final else 3
    units = [(bi, h) for bi in range(nb) for h in range(HG_HEADS)]
    pairs = [(t, s) for t in range(HG_SUB) for s in range(HG_SUB) if ((s >= t) if reverse else (s <= t))]

    def chunk_body(i, carry):
        c = (n_chunks - 1 - i) if reverse else i
        rows = pl.ds(pl.multiple_of(c * ch, ch), ch)
        parts = []
        for bi in range(nb):
            parts += [hq_ref[bi, rows, :], hf_ref[bi, rows, :], hi_ref[bi, rows, :]]
            if final:
                parts.append(hg_ref[bi, rows, :])
        xp = jnp.dot(perm_ref[...], jnp.concatenate(parts, axis=1), preferred_element_type=F32)

        def col(group, bi, h):
            c0 = (bi * n_groups + group) * HG_WIDTH + h * HG_DK
            return xp[:, c0:c0 + HG_DK]

        q, kk, qb, kb, v, vb, g_hi, g_lo = [], [], [], [], [], [], [], []
        for bi, h in units:
            cols = slice(h * HG_DK, (h + 1) * HG_DK)
            z = col(1, bi, h)
            e = jnp.exp2(-LOG2_E * jnp.abs(z))
            r = 1.0 / (1.0 + e)
            er = e * r
            pos = z >= 0.0
            g = jnp.log2(lb_floor[:, cols] + one_m_lb[:, cols] * jnp.where(pos, r, er))
            kk.append(one_m_lb[:, cols] * jnp.where(pos, er, r))
            q.append(_silu_exp2(col(0, bi, h)))
            qb.append(q[-1].astype(BF16))
            kb.append(kk[-1].astype(BF16))
            v.append(col(2, bi, h))
            vb.append(col(2, bi, h).astype(BF16))
            g_hi.append(g.astype(BF16))
            g_lo.append((g - g_hi[-1].astype(F32)).astype(BF16))

        g2 = jnp.concatenate([jnp.concatenate(g_hi, axis=1), jnp.concatenate(g_lo, axis=1)], axis=0)
        y_all = jnp.dot(w_ref[...], g2, preferred_element_type=F32)
        y = [y_all[:, u * HG_DK:(u + 1) * HG_DK] for u in range(len(units))]
        b = [yu[0:ch] for yu in y]
        b_end = [bu[0:1] if reverse else bu[ch - 1:ch] for bu in b]

        a = []
        for u in range(len(units)):
            au = None
            for l in range(n_lev):
                ref = y[u][ch + l * nsb:ch + (l + 1) * nsb]
                el = jnp.exp2(-jnp.abs(b[u] - jnp.concatenate([ref] * HG_SUB, axis=0))).astype(BF16)
                al = lax.dot_general(qb[u] * el, kb[u] * el, NT_DIMS,
                                     preferred_element_type=F32) * mask_ref[l]
                au = al if au is None else au + al
            a.append(au.astype(BF16))

        pieces = []
        for u in range(len(units)):
            for t, s in pairs:
                p = slab(qb[u], t) * slab(kb[u], s)
                if s != t:
                    p = p * jnp.exp2(slab(b[u], t) - slab(b[u], s)).astype(BF16)
                pieces.append(p)
        rsum = jnp.dot(jnp.concatenate(pieces, axis=0), ones, preferred_element_type=F32)

        outs = []
        for u, (bi, h) in enumerate(units):
            cols = slice(h * HG_DK, (h + 1) * HG_DK)
            od = [None] * HG_SUB
            for n, (t, s) in enumerate(pairs):
                n0 = (u * len(pairs) + n) * nsb
                term = rsum[n0:n0 + nsb] * slab(v[u], s)
                od[t] = term if od[t] is None else od[t] + term
            qs = (q[u] * jnp.exp2(b[u])).astype(BF16)
            ks = (kk[u] * jnp.exp2(b_end[u] - b[u])).astype(BF16)
            st = st_ref[bi, h]
            o = (jnp.dot(a[u], vb[u], preferred_element_type=F32) + jnp.concatenate(od, axis=0)
                 + lax.dot_general(qs, st.astype(BF16), NT_DIMS, preferred_element_type=F32))
            st_ref[bi, h] = (st * jnp.exp2(b_end[u])
                             + lax.dot_general(vb[u], ks, TN_DIMS, preferred_element_type=F32))
            if final:
                o = o + of_ref[bi, rows, cols]
                o = _rms(o, gw_ref[...], NORM_EPS) * _silu_exp2(col(3, bi, h))
                outs.append(o.astype(BF16))
            else:
                o_ref[bi, rows, cols] = o
        if final:
            y_out = jnp.dot(permt_ref[...], jnp.concatenate(outs, axis=1), preferred_element_type=F32)
            for bi in range(nb):
                o_ref[bi, rows, :] = y_out[:, bi * HG_WIDTH:(bi + 1) * HG_WIDTH].astype(o_ref.dtype)
        return carry

    lax.fori_loop(0, n_chunks, chunk_body, 0, unroll=HG_UNROLL)


def _hgrn_call(proj, lb_dir, gnorm_w, o_fwd, *, reverse):
    b, t, _ = proj.shape
    final = o_fwd is not None
    ch = HG_CHUNK
    tile = min(HG_TILE, t)
    assert t % tile == 0 and tile % ch == 0
    nt = t // tile
    perm_np, w_np, mask_np = _hgrn_constants(ch, reverse)

    def tok(j):
        return (nt - 1 - j) if reverse else j

    nb = HG_ROWS if b % HG_ROWS == 0 else 1

    def col_spec(group):
        return pl.BlockSpec((nb, tile, GROUP), lambda i, j: (i, tok(j), group))

    def const_spec(shape):
        return pl.BlockSpec(shape, lambda i, j: (0,) * len(shape))

    in_specs = [col_spec(COL_HQ), col_spec(COL_HFB if reverse else COL_HFF), col_spec(COL_HI)]
    args = [proj, proj, proj]
    if final:
        in_specs += [col_spec(COL_HG), pl.BlockSpec((nb, tile, HG_WIDTH), lambda i, j: (i, tok(j), 0))]
        args += [proj, o_fwd]
    in_specs.append(const_spec((1, HG_WIDTH)))
    args.append(lb_dir.reshape(1, HG_WIDTH))
    if final:
        in_specs.append(const_spec((1, HG_DK)))
        args.append(gnorm_w.reshape(1, HG_DK))
    in_specs.append(const_spec(perm_np.shape))
    args.append(jnp.asarray(perm_np, BF16))
    if final:
        in_specs.append(const_spec(perm_np.shape))
        args.append(jnp.asarray(perm_np.T, BF16))
    in_specs += [const_spec(w_np.shape), const_spec(mask_np.shape)]
    args += [jnp.asarray(w_np, BF16), jnp.asarray(mask_np, F32)]

    kern = functools.partial(_hgrn_kernel, reverse=reverse, final=final, ch=ch, n_chunks=tile // ch)
    return pl.pallas_call(
        kern,
        out_shape=jax.ShapeDtypeStruct((b, t, HG_WIDTH), BF16 if final else F32),
        grid=(b // nb, nt),
        in_specs=in_specs,
        out_specs=pl.BlockSpec((nb, tile, HG_WIDTH), lambda i, j: (i, tok(j), 0)),
        scratch_shapes=[pltpu.VMEM((nb, HG_HEADS, HG_DK, HG_DK), F32)],
        compiler_params=pltpu.CompilerParams(
            dimension_semantics=("parallel", "arbitrary"), vmem_limit_bytes=VMEM_LIMIT_BYTES),
        name="hgrn2_bwd" if reverse else "hgrn2_fwd",
    )(*args)


def _attn_kernel(q_ref, k_ref, v_ref, lq1_ref, lk1_ref, lq2_ref, lk2_ref, sw_ref, o_ref,
                 *, tq, tkc, n_kv, n_sub, lambda_init):
    for u in range(n_sub):
        _attn_tile(q_ref, k_ref, v_ref, lq1_ref, lk1_ref, lq2_ref, lk2_ref, sw_ref, o_ref,
                   slice(u * tq, (u + 1) * tq), tq=tq, tkc=tkc, n_kv=n_kv, lambda_init=lambda_init)


def _attn_tile(q_ref, k_ref, v_ref, lq1_ref, lk1_ref, lq2_ref, lk2_ref, sw_ref, o_ref, q_rows,
               *, tq, tkc, n_kv, lambda_init):
    q = q_ref[0, q_rows, :]
    lane = lax.broadcasted_iota(jnp.int32, q.shape, 1)
    zero = jnp.zeros_like(q)
    qs = jnp.concatenate([jnp.where(lane < DA_HEAD_DIM, q, zero),
                          jnp.where(lane >= DA_HEAD_DIM, q, zero)], axis=0)

    def body(j, carry):
        m_prev, l_prev, acc = carry
        rows = pl.ds(pl.multiple_of(j * tkc, tkc), tkc)
        k = k_ref[0, rows, :]
        v = v_ref[0, rows, :]
        s = lax.dot_general(qs, k, NT_DIMS, preferred_element_type=F32)
        m_new = jnp.maximum(m_prev, jnp.max(s, axis=-1, keepdims=True))
        alpha = jnp.exp2(m_prev - m_new)
        p = jnp.exp2(s - m_new)
        l_new = alpha * l_prev + jnp.sum(p, axis=-1, keepdims=True)
        acc = alpha * acc + jnp.dot(p.astype(BF16), v, preferred_element_type=F32)
        return m_new, l_new, acc

    init = (jnp.full((2 * tq, 1), -jnp.inf, F32), jnp.zeros((2 * tq, 1), F32),
            jnp.zeros((2 * tq, 2 * DA_HEAD_DIM), F32))
    _, l_fin, acc = lax.fori_loop(0, n_kv, body, init, unroll=True)

    o = acc / l_fin
    lam = (jnp.exp(jnp.sum(lq1_ref[...] * lk1_ref[...], axis=-1, keepdims=True))
           - jnp.exp(jnp.sum(lq2_ref[...] * lk2_ref[...], axis=-1, keepdims=True)) + lambda_init)
    od = o[0:tq] - lam * o[tq:2 * tq]
    od = _rms(od, sw_ref[...], SUBLN_EPS) * (1.0 - lambda_init)
    o_ref[0, q_rows, :] = od.astype(o_ref.dtype)


def _attn_call(proj, lq1, lk1, lq2, lk2, subln_w, lambda_init):
    b, t, _ = proj.shape
    tq = min(ATT_TQ, t)
    tkc = min(ATT_TK, t)
    n_sub = ATT_SUB if t % (ATT_SUB * tq) == 0 else 1
    assert t % (n_sub * tq) == 0 and t % tkc == 0
    head_w = 2 * DA_HEAD_DIM
    blocks_per_group = GROUP // head_w

    def lam_spec():
        return pl.BlockSpec((1, DA_HEAD_DIM), lambda i, h, qi: (0, 0))

    kern = functools.partial(_attn_kernel, tq=tq, tkc=tkc, n_kv=t // tkc, n_sub=n_sub, lambda_init=lambda_init)
    return pl.pallas_call(
        kern,
        out_shape=jax.ShapeDtypeStruct((b, t, DA_WIDTH), BF16),
        grid=(b, DA_HEADS, t // (n_sub * tq)),
        in_specs=[
            pl.BlockSpec((1, n_sub * tq, head_w), lambda i, h, qi: (i, qi, COL_DQ * blocks_per_group + h)),
            pl.BlockSpec((1, t, head_w), lambda i, h, qi: (i, 0, COL_DK * blocks_per_group + h)),
            pl.BlockSpec((1, t, head_w), lambda i, h, qi: (i, 0, COL_DV * blocks_per_group + h)),
            lam_spec(), lam_spec(), lam_spec(), lam_spec(),
            pl.BlockSpec((1, head_w), lambda i, h, qi: (0, 0)),
        ],
        out_specs=pl.BlockSpec((1, n_sub * tq, head_w), lambda i, h, qi: (i, qi, h)),
        compiler_params=pltpu.CompilerParams(
            dimension_semantics=("parallel", "parallel", "arbitrary"),
            vmem_limit_bytes=VMEM_LIMIT_BYTES),
        name="diff_attn",
    )(proj, proj, proj, lq1.reshape(1, -1), lk1.reshape(1, -1), lq2.reshape(1, -1), lk2.reshape(1, -1),
      subln_w.reshape(1, -1))


def _outffn_kernel(x_ref, ohg_ref, oda_ref, mod_ref, nw_ref, wo_ref, wg_ref, wu_ref, wd_ref, o_ref):
    rows_part = x_ref.shape[1] // FFN_PRO_PARTS
    x1_parts, h_parts = [], []
    for u in range(FFN_PRO_PARTS):
        rows = slice(u * rows_part, (u + 1) * rows_part)
        mix = (jnp.dot(ohg_ref[0, rows, :], wo_ref[0:HG_WIDTH, :], preferred_element_type=F32)
               + jnp.dot(oda_ref[0, rows, :], wo_ref[HG_WIDTH:HG_WIDTH + DA_WIDTH, :],
                         preferred_element_type=F32))
        x1u = x_ref[0, rows, :] + mod_ref[0, 2:3, :] * _rms(mix, nw_ref[0:1, :], NORM_EPS)
        hu = _rms(x1u, nw_ref[1:2, :], NORM_EPS) * (1.0 + mod_ref[0, 4:5, :]) + mod_ref[0, 3:4, :]
        x1_parts.append(x1u)
        h_parts.append(hu.astype(BF16))
    x1 = jnp.concatenate(x1_parts, axis=0)
    h = jnp.concatenate(h_parts, axis=0)
    hidden = wg_ref.shape[1]
    f = None
    for c in range(hidden // FFN_CHUNK):
        cs = slice(c * FFN_CHUNK, (c + 1) * FFN_CHUNK)
        g = jnp.dot(h, wg_ref[:, cs], preferred_element_type=F32)
        u = jnp.dot(h, wu_ref[:, cs], preferred_element_type=F32)
        a = (_silu_exp2(g) * u).astype(BF16)
        fc = jnp.dot(a, wd_ref[cs, :], preferred_element_type=F32)
        f = fc if f is None else f + fc
    o_ref[0] = x1 + mod_ref[0, 5:6, :] * _rms(f, nw_ref[2:3, :], NORM_EPS)


def _outffn_call(x, o_hg, o_da, mod, norms, wo, wg, wu, wd):
    b, t, d = x.shape
    hidden = wg.shape[1]
    tm = min(TM_FFN, t)
    assert t % tm == 0 and hidden % FFN_CHUNK == 0

    def resident(shape):
        return pl.BlockSpec(shape, lambda i, j: (0, 0), pipeline_mode=pl.Buffered(1))

    return pl.pallas_call(
        _outffn_kernel,
        out_shape=jax.ShapeDtypeStruct((b, t, d), F32),
        grid=(b, t // tm),
        in_specs=[
            pl.BlockSpec((1, tm, d), lambda i, j: (i, j, 0)),
            pl.BlockSpec((1, tm, HG_WIDTH), lambda i, j: (i, j, 0)),
            pl.BlockSpec((1, tm, DA_WIDTH), lambda i, j: (i, j, 0)),
            pl.BlockSpec((1, N_MOD, d), lambda i, j: (i, 0, 0)),
            pl.BlockSpec((3, d), lambda i, j: (0, 0)),
            resident((HG_WIDTH + DA_WIDTH, d)),
            resident((d, hidden)),
            resident((d, hidden)),
            resident((hidden, d)),
        ],
        out_specs=pl.BlockSpec((1, tm, d), lambda i, j: (i, j, 0)),
        compiler_params=pltpu.CompilerParams(
            dimension_semantics=("parallel", "parallel"), vmem_limit_bytes=VMEM_LIMIT_BYTES),
        name="outproj_ffn",
    )(x, o_hg, o_da, mod, norms, wo, wg, wu, wd)


def _rotary_tables(t):
    dh = DA_HEAD_DIM
    inv = 1.0 / (ROPE_THETA ** (jnp.arange(0, dh, 2, dtype=F32) / dh))
    ang = jnp.arange(t, dtype=F32)[:, None] * inv[None, :]
    cos = jnp.cos(ang)
    sin = jnp.sin(ang)
    reps = LANES // dh
    cos_t = jnp.tile(jnp.concatenate([cos, cos], axis=-1), (1, reps))
    sin_t = jnp.tile(jnp.concatenate([-sin, sin], axis=-1), (1, reps))
    return cos_t, sin_t


def kernel(x_prompt, x_sample, c_prompt, c_sample, w_ada, b_ada, norm_pre_mix, norm_post_mix, norm_pre_ffn, norm_post_ffn, w_in, hg_lower_bounds, hg_gnorm, da_lambda_q1, da_lambda_k1, da_lambda_q2, da_lambda_k2, da_subln, w_out, w_ffn_gate, w_ffn_up, w_ffn_down):
    depth = w_in.shape[0]
    d = x_prompt.shape[-1]

    lb_soft = jax.nn.softmax(hg_lower_bounds.astype(F32), axis=1)
    lb_all = jnp.cumsum(lb_soft, axis=1) - lb_soft[:, :1]

    c_all = jnp.concatenate([c_prompt, c_sample], axis=0)
    mod_all = _ada_call(c_all, w_ada, b_ada)

    w_in_b = w_in.astype(BF16)
    w_out_b = w_out.astype(BF16)
    wg_b = w_ffn_gate.astype(BF16)
    wu_b = w_ffn_up.astype(BF16)
    wd_b = w_ffn_down.astype(BF16)

    def trunk(x, row0):
        b, t, _ = x.shape
        cos, sin = _rotary_tables(t)
        for l in range(depth):
            mod = mod_all[l, row0:row0 + b].reshape(b, N_MOD, d)
            lambda_init = 0.8 - 0.6 * math.exp(-0.3 * l)
            proj = _inproj_call(x, mod, norm_pre_mix[l], w_in_b[l], cos, sin)
            o_f = _hgrn_call(proj, lb_all[0, l], None, None, reverse=False)
            o_hg = _hgrn_call(proj, lb_all[1, l], hg_gnorm[l], o_f, reverse=True)
            o_da = _attn_call(proj, da_lambda_q1[l], da_lambda_k1[l], da_lambda_q2[l], da_lambda_k2[l],
                              da_subln[l], lambda_init)
            norms = jnp.stack([norm_post_mix[l], norm_pre_ffn[l], norm_post_ffn[l]], axis=0)
            x = _outffn_call(x, o_hg, o_da, mod, norms, w_out_b[l], wg_b[l], wu_b[l], wd_b[l])
        return x

    y_prompt = trunk(x_prompt, 0)
    y_sample = trunk(x_sample, c_prompt.shape[0])
    return (y_prompt, y_sample)
```

```python
import functools
import math

import numpy as np
import jax
import jax.numpy as jnp
from jax import lax
from jax.experimental import pallas as pl
from jax.experimental.pallas import tpu as pltpu

F32 = jnp.float32
BF16 = jnp.bfloat16

HG_WIDTH = 512
HG_HEADS = 4
HG_DK = 128
DA_WIDTH = 512
DA_HEADS = 4
DA_HEAD_DIM = 64
ROPE_THETA = 10000.0
NORM_EPS = 1e-6
SUBLN_EPS = 1e-5
LOG_FLOOR = 1e-30
LOG2_E = math.log2(math.e)
N_MOD = 6

LANES = 128
SUBLANES = 8
VMEM_LIMIT_BYTES = 56 * 1024 * 1024

COL_HQ, COL_HFF, COL_HFB, COL_HI, COL_HG, COL_DQ, COL_DK, COL_DV = range(8)
GROUP = 512

ADA_TN = 1536
TM_PROJ = 1024
TM_FFN = 1024
FFN_PRO_PARTS = 4
FFN_CHUNK = 256
HG_CHUNK = 128
HG_TILE = 512
HG_SUB = SUBLANES
HG_ROWS = 4
HG_UNROLL = 2
ATT_TQ = 512
ATT_TK = 2048
ATT_SUB = 2

NT_DIMS = (((1,), (1,)), ((), ()))
TN_DIMS = (((0,), (0,)), ((), ()))


def _silu(x):
    return x * jax.nn.sigmoid(x)


def _rms(x, w, eps):
    ms = jnp.mean(x * x, axis=-1, keepdims=True)
    return x * lax.rsqrt(ms + eps) * w


def _ada_kernel(c_ref, w_ref, b_ref, o_ref):
    s = _silu(c_ref[...]).astype(BF16)
    w = w_ref[0].astype(BF16)
    o_ref[0] = jnp.dot(s, w, preferred_element_type=F32) + b_ref[0]


def _ada_call(c_all, w_ada, b_ada):
    depth, d, n = w_ada.shape
    rows = c_all.shape[0]
    tn = ADA_TN
    assert n % tn == 0
    return pl.pallas_call(
        _ada_kernel,
        out_shape=jax.ShapeDtypeStruct((depth, rows, n), F32),
        grid=(depth, n // tn),
        in_specs=[
            pl.BlockSpec((rows, d), lambda l, j: (0, 0)),
            pl.BlockSpec((1, d, tn), lambda l, j: (l, 0, j)),
            pl.BlockSpec((1, 1, tn), lambda l, j: (l, 0, j)),
        ],
        out_specs=pl.BlockSpec((1, rows, tn), lambda l, j: (l, 0, j)),
        compiler_params=pltpu.CompilerParams(
            dimension_semantics=("parallel", "parallel"), vmem_limit_bytes=VMEM_LIMIT_BYTES),
        name="adaln_mod",
    )(c_all, w_ada, b_ada.reshape(depth, 1, n))


def _inproj_kernel(x_ref, mod_ref, nw_ref, w_ref, cos_ref, sin_ref, o_ref):
    x = x_ref[0]
    h = _rms(x, nw_ref[...], NORM_EPS) * (1.0 + mod_ref[0, 1:2, :]) + mod_ref[0, 0:1, :]
    h = h.astype(BF16)
    cos = cos_ref[...]
    sin = sin_ref[...]
    lane = lax.broadcasted_iota(jnp.int32, cos.shape, 1)
    first_half = (lane % DA_HEAD_DIM) < (DA_HEAD_DIM // 2)
    n_groups = w_ref.shape[1] // GROUP
    for n in range(n_groups):
        y = jnp.dot(h, w_ref[:, n * GROUP:(n + 1) * GROUP], preferred_element_type=F32)
        if n in (COL_DQ, COL_DK):
            pieces = []
            for j in range(GROUP // LANES):
                yj = y[:, j * LANES:(j + 1) * LANES]
                partner = jnp.where(first_half,
                                    pltpu.roll(yj, LANES - DA_HEAD_DIM // 2, 1),
                                    pltpu.roll(yj, DA_HEAD_DIM // 2, 1))
                r = yj * cos + partner * sin
                if n == COL_DQ:
                    r = r * (DA_HEAD_DIM ** -0.5 * LOG2_E)
                pieces.append(r)
            y = jnp.concatenate(pieces, axis=1)
        o_ref[0, :, n * GROUP:(n + 1) * GROUP] = y.astype(BF16)


def _inproj_call(x, mod, nw, w_bf16, cos, sin):
    b, t, d = x.shape
    n = w_bf16.shape[1]
    tm = min(TM_PROJ, t)
    assert t % tm == 0
    return pl.pallas_call(
        _inproj_kernel,
        out_shape=jax.ShapeDtypeStruct((b, t, n), BF16),
        grid=(b, t // tm),
        in_specs=[
            pl.BlockSpec((1, tm, d), lambda i, j: (i, j, 0)),
            pl.BlockSpec((1, N_MOD, d), lambda i, j: (i, 0, 0)),
            pl.BlockSpec((1, d), lambda i, j: (0, 0)),
            pl.BlockSpec((d, n), lambda i, j: (0, 0), pipeline_mode=pl.Buffered(1)),
            pl.BlockSpec((tm, LANES), lambda i, j: (j, 0)),
            pl.BlockSpec((tm, LANES), lambda i, j: (j, 0)),
        ],
        out_specs=pl.BlockSpec((1, tm, n), lambda i, j: (i, j, 0)),
        compiler_params=pltpu.CompilerParams(
            dimension_semantics=("parallel", "parallel"), vmem_limit_bytes=VMEM_LIMIT_BYTES),
        name="norm_inproj",
    )(x, mod, nw.reshape(1, d), w_bf16, cos, sin)


def _hgrn_levels(ch):
    levels = []
    m = HG_SUB
    while m < ch:
        levels.append(m)
        m *= 2
    return levels


@functools.lru_cache(maxsize=None)
def _hgrn_constants(ch, reverse):
    idx = np.arange(ch)
    nsb = ch // HG_SUB
    order = (idx % nsb) * HG_SUB + idx // nsb
    if reverse:
        cum = (idx[None, :] >= idx[:, None]).astype(np.float32)
    else:
        cum = (idx[None, :] <= idx[:, None]).astype(np.float32)
    refs = []
    masks = []
    for m in _hgrn_levels(ch):
        pair = idx // (2 * m)
        in_second = (idx % (2 * m)) >= m
        ref = pair * 2 * m + (m if reverse else m - 1)
        refs.append(cum[ref[::HG_SUB]])
        later = ~in_second if reverse else in_second
        earlier = ~later
        masks.append(((pair[:, None] == pair[None, :]) & later[:, None] & earlier[None, :]).astype(np.float32))
    sel = np.ix_(order, order)
    w = np.concatenate([cum[sel]] + [rf[:, order] for rf in refs], axis=0)
    w = np.concatenate([w, w], axis=1)
    masks = np.stack([mk[sel] for mk in masks], axis=0)
    perm = np.eye(ch, dtype=np.float32)[order]
    return perm, w, masks


def _silu_exp2(x):
    return x / (1.0 + jnp.exp2(-LOG2_E * x))


def _hgrn_kernel(*refs, reverse, final, ch, n_chunks):
    if final:
        (hq_ref, hf_ref, hi_ref, hg_ref, of_ref, lb_ref, gw_ref, perm_ref, permt_ref, w_ref, mask_ref,
         o_ref, st_ref) = refs
    else:
        (hq_ref, hf_ref, hi_ref, lb_ref, perm_ref, w_ref, mask_ref, o_ref, st_ref) = refs
    n_lev = len(_hgrn_levels(ch))
    nsb = ch // HG_SUB

    @pl.when(pl.program_id(1) == 0)
    def _():
        st_ref[...] = jnp.zeros_like(st_ref)

    lb = lb_ref[...]
    lb_floor = jnp.maximum(lb, LOG_FLOOR)
    one_m_lb = 1.0 - lb
    ones = jnp.ones((LANES, LANES), BF16)

    def slab(x, t):
        return x[t * nsb:(t + 1) * nsb]

    nb = hq_ref.shape[0]
    n_groups = 4 if final else 3
    units = [(bi, h) for bi in range(nb) for h in range(HG_HEADS)]
    pairs = [(t, s) for t in range(HG_SUB) for s in range(HG_SUB) if ((s >= t) if reverse else (s <= t))]

    def chunk_body(i, carry):
        c = (n_chunks - 1 - i) if reverse else i
        rows = pl.ds(pl.multiple_of(c * ch, ch), ch)
        parts = []
        for bi in range(nb):
            parts += [hq_ref[bi, rows, :], hf_ref[bi, rows, :], hi_ref[bi, rows, :]]
            if final:
                parts.append(hg_ref[bi, rows, :])
        xp = jnp.dot(perm_ref[...], jnp.concatenate(parts, axis=1), preferred_element_type=F32)

        def col(group, bi, h):
            c0 = (bi * n_groups + group) * HG_WIDTH + h * HG_DK
            return xp[:, c0:c0 + HG_DK]

        q, kk, qb, kb, v, vb, g_hi, g_lo = [], [], [], [], [], [], [], []
        for bi, h in units:
            cols = slice(h * HG_DK, (h + 1) * HG_DK)
            z = col(1, bi, h)
            e = jnp.exp2(-LOG2_E * jnp.abs(z))
            r = 1.0 / (1.0 + e)
            er = e * r
            pos = z >= 0.0
            g = jnp.log2(lb_floor[:, cols] + one_m_lb[:, cols] * jnp.where(pos, r, er))
            kk.append(one_m_lb[:, cols] * jnp.where(pos, er, r))
            q.append(_silu_exp2(col(0, bi, h)))
            qb.append(q[-1].astype(BF16))
            kb.append(kk[-1].astype(BF16))
            v.append(col(2, bi, h))
            vb.append(col(2, bi, h).astype(BF16))
            g_hi.append(g.astype(BF16))
            g_lo.append((g - g_hi[-1].astype(F32)).astype(BF16))

        g2 = jnp.concatenate([jnp.concatenate(g_hi, axis=1), jnp.concatenate(g_lo, axis=1)], axis=0)
        y_all = jnp.dot(w_ref[...], g2, preferred_element_type=F32)
        y = [y_all[:, u * HG_DK:(u + 1) * HG_DK] for u in range(len(units))]
        b = [yu[0:ch] for yu in y]
        b_end = [bu[0:1] if reverse else bu[ch - 1:ch] for bu in b]

        a = []
        for u in range(len(units)):
            au = None
            for l in range(n_lev):
                ref = y[u][ch + l * nsb:ch + (l + 1) * nsb]
                el = jnp.exp2(-jnp.abs(b[u] - jnp.concatenate([ref] * HG_SUB, axis=0))).astype(BF16)
                al = lax.dot_general(qb[u] * el, kb[u] * el, NT_DIMS,
                                     preferred_element_type=F32) * mask_ref[l]
                au = al if au is None else au + al
            a.append(au.astype(BF16))

        pieces = []
        for u in range(len(units)):
            for t, s in pairs:
                p = slab(qb[u], t) * slab(kb[u], s)
                if s != t:
                    p = p * jnp.exp2(slab(b[u], t) - slab(b[u], s)).astype(BF16)
                pieces.append(p)
        rsum = jnp.dot(jnp.concatenate(pieces, axis=0), ones, preferred_element_type=F32)

        outs = []
        for u, (bi, h) in enumerate(units):
            cols = slice(h * HG_DK, (h + 1) * HG_DK)
            od = [None] * HG_SUB
            for n, (t, s) in enumerate(pairs):
                n0 = (u * len(pairs) + n) * nsb
                term = rsum[n0:n0 + nsb] * slab(v[u], s)
                od[t] = term if od[t] is None else od[t] + term
            qs = (q[u] * jnp.exp2(b[u])).astype(BF16)
            ks = (kk[u] * jnp.exp2(b_end[u] - b[u])).astype(BF16)
            st = st_ref[bi, h]
            o = (jnp.dot(a[u], vb[u], preferred_element_type=F32) + jnp.concatenate(od, axis=0)
                 + lax.dot_general(qs, st.astype(BF16), NT_DIMS, preferred_element_type=F32))
            st_ref[bi, h] = (st * jnp.exp2(b_end[u])
                             + lax.dot_general(vb[u], ks, TN_DIMS, preferred_element_type=F32))
            if final:
                o = o + of_ref[bi, rows, cols]
                o = _rms(o, gw_ref[...], NORM_EPS) * _silu_exp2(col(3, bi, h))
                outs.append(o.astype(BF16))
            else:
                o_ref[bi, rows, cols] = o
        if final:
            y_out = jnp.dot(permt_ref[...], jnp.concatenate(outs, axis=1), preferred_element_type=F32)
            for bi in range(nb):
                o_ref[bi, rows, :] = y_out[:, bi * HG_WIDTH:(bi + 1) * HG_WIDTH].astype(o_ref.dtype)
        return carry

    lax.fori_loop(0, n_chunks, chunk_body, 0, unroll=HG_UNROLL)


def _hgrn_call(proj, lb_dir, gnorm_w, o_fwd, *, reverse):
    b, t, _ = proj.shape
    final = o_fwd is not None
    ch = HG_CHUNK
    tile = min(HG_TILE, t)
    assert t % tile == 0 and tile % ch == 0
    nt = t // tile
    perm_np, w_np, mask_np = _hgrn_constants(ch, reverse)

    def tok(j):
        return (nt - 1 - j) if reverse else j

    nb = HG_ROWS if b % HG_ROWS == 0 else 1

    def col_spec(group):
        return pl.BlockSpec((nb, tile, GROUP), lambda i, j: (i, tok(j), group))

    def const_spec(shape):
        return pl.BlockSpec(shape, lambda i, j: (0,) * len(shape))

    in_specs = [col_spec(COL_HQ), col_spec(COL_HFB if reverse else COL_HFF), col_spec(COL_HI)]
    args = [proj, proj, proj]
    if final:
        in_specs += [col_spec(COL_HG), pl.BlockSpec((nb, tile, HG_WIDTH), lambda i, j: (i, tok(j), 0))]
        args += [proj, o_fwd]
    in_specs.append(const_spec((1, HG_WIDTH)))
    args.append(lb_dir.reshape(1, HG_WIDTH))
    if final:
        in_specs.append(const_spec((1, HG_DK)))
        args.append(gnorm_w.reshape(1, HG_DK))
    in_specs.append(const_spec(perm_np.shape))
    args.append(jnp.asarray(perm_np, BF16))
    if final:
        in_specs.append(const_spec(perm_np.shape))
        args.append(jnp.asarray(perm_np.T, BF16))
    in_specs += [const_spec(w_np.shape), const_spec(mask_np.shape)]
    args += [jnp.asarray(w_np, BF16), jnp.asarray(mask_np, F32)]

    kern = functools.partial(_hgrn_kernel, reverse=reverse, final=final, ch=ch, n_chunks=tile // ch)
    return pl.pallas_call(
        kern,
        out_shape=jax.ShapeDtypeStruct((b, t, HG_WIDTH), BF16 if final else F32),
        grid=(b // nb, nt),
        in_specs=in_specs,
        out_specs=pl.BlockSpec((nb, tile, HG_WIDTH), lambda i, j: (i, tok(j), 0)),
        scratch_shapes=[pltpu.VMEM((nb, HG_HEADS, HG_DK, HG_DK), F32)],
        compiler_params=pltpu.CompilerParams(
            dimension_semantics=("parallel", "arbitrary"), vmem_limit_bytes=VMEM_LIMIT_BYTES),
        name="hgrn2_bwd" if reverse else "hgrn2_fwd",
    )(*args)


def _attn_kernel(q_ref, k_ref, v_ref, lq1_ref, lk1_ref, lq2_ref, lk2_ref, sw_ref, o_ref,
                 *, tq, tkc, n_kv, n_sub, lambda_init):
    for u in range(n_sub):
        _attn_tile(q_ref, k_ref, v_ref, lq1_ref, lk1_ref, lq2_ref, lk2_ref, sw_ref, o_ref,
                   slice(u * tq, (u + 1) * tq), tq=tq, tkc=tkc, n_kv=n_kv, lambda_init=lambda_init)


def _attn_tile(q_ref, k_ref, v_ref, lq1_ref, lk1_ref, lq2_ref, lk2_ref, sw_ref, o_ref, q_rows,
               *, tq, tkc, n_kv, lambda_init):
    q = q_ref[0, q_rows, :]
    lane = lax.broadcasted_iota(jnp.int32, q.shape, 1)
    zero = jnp.zeros_like(q)
    qs = jnp.concatenate([jnp.where(lane < DA_HEAD_DIM, q, zero),
                          jnp.where(lane >= DA_HEAD_DIM, q, zero)], axis=0)

    def body(j, carry):
        m_prev, l_prev, acc = carry
        rows = pl.ds(pl.multiple_of(j * tkc, tkc), tkc)
        k = k_ref[0, rows, :]
        v = v_ref[0, rows, :]
        s = lax.dot_general(qs, k, NT_DIMS, preferred_element_type=F32)
        m_new = jnp.maximum(m_prev, jnp.max(s, axis=-1, keepdims=True))
        alpha = jnp.exp2(m_prev - m_new)
        p = jnp.exp2(s - m_new)
        l_new = alpha * l_prev + jnp.sum(p, axis=-1, keepdims=True)
        acc = alpha * acc + jnp.dot(p.astype(BF16), v, preferred_element_type=F32)
        return m_new, l_new, acc

    init = (jnp.full((2 * tq, 1), -jnp.inf, F32), jnp.zeros((2 * tq, 1), F32),
            jnp.zeros((2 * tq, 2 * DA_HEAD_DIM), F32))
    _, l_fin, acc = lax.fori_loop(0, n_kv, body, init, unroll=True)

    o = acc / l_fin
    lam = (jnp.exp(jnp.sum(lq1_ref[...] * lk1_ref[...], axis=-1, keepdims=True))
           - jnp.exp(jnp.sum(lq2_ref[...] * lk2_ref[...], axis=-1, keepdims=True)) + lambda_init)
    od = o[0:tq] - lam * o[tq:2 * tq]
    od = _rms(od, sw_ref[...], SUBLN_EPS) * (1.0 - lambda_init)
    o_ref[0, q_rows, :] = od.astype(o_ref.dtype)


def _attn_call(proj, lq1, lk1, lq2, lk2, subln_w, lambda_init):
    b, t, _ = proj.shape
    tq = min(ATT_TQ, t)
    tkc = min(ATT_TK, t)
    n_sub = ATT_SUB if t % (ATT_SUB * tq) == 0 else 1
    assert t % (n_sub * tq) == 0 and t % tkc == 0
    head_w = 2 * DA_HEAD_DIM
    blocks_per_group = GROUP // head_w

    def lam_spec():
        return pl.BlockSpec((1, DA_HEAD_DIM), lambda i, h, qi: (0, 0))

    kern = functools.partial(_attn_kernel, tq=tq, tkc=tkc, n_kv=t // tkc, n_sub=n_sub, lambda_init=lambda_init)
    return pl.pallas_call(
        kern,
        out_shape=jax.ShapeDtypeStruct((b, t, DA_WIDTH), BF16),
        grid=(b, DA_HEADS, t // (n_sub * tq)),
        in_specs=[
            pl.BlockSpec((1, n_sub * tq, head_w), lambda i, h, qi: (i, qi, COL_DQ * blocks_per_group + h)),
            pl.BlockSpec((1, t, head_w), lambda i, h, qi: (i, 0, COL_DK * blocks_per_group + h)),
            pl.BlockSpec((1, t, head_w), lambda i, h, qi: (i, 0, COL_DV * blocks_per_group + h)),
            lam_spec(), lam_spec(), lam_spec(), lam_spec(),
            pl.BlockSpec((1, head_w), lambda i, h, qi: (0, 0)),
        ],
        out_specs=pl.BlockSpec((1, n_sub * tq, head_w), lambda i, h, qi: (i, qi, h)),
        compiler_params=pltpu.CompilerParams(
            dimension_semantics=("parallel", "parallel", "arbitrary"),
            vmem_limit_bytes=VMEM_LIMIT_BYTES),
        name="diff_attn",
    )(proj, proj, proj, lq1.reshape(1, -1), lk1.reshape(1, -1), lq2.reshape(1, -1), lk2.reshape(1, -1),
      subln_w.reshape(1, -1))


def _outffn_kernel(x_ref, ohg_ref, oda_ref, mod_ref, nw_ref, wo_ref, wg_ref, wu_ref, wd_ref, o_ref):
    rows_part = x_ref.shape[1] // FFN_PRO_PARTS
    x1_parts, h_parts = [], []
    for u in range(FFN_PRO_PARTS):
        rows = slice(u * rows_part, (u + 1) * rows_part)
        mix = (jnp.dot(ohg_ref[0, rows, :], wo_ref[0:HG_WIDTH, :], preferred_element_type=F32)
               + jnp.dot(oda_ref[0, rows, :], wo_ref[HG_WIDTH:HG_WIDTH + DA_WIDTH, :],
                         preferred_element_type=F32))
        x1u = x_ref[0, rows, :] + mod_ref[0, 2:3, :] * _rms(mix, nw_ref[0:1, :], NORM_EPS)
        hu = _rms(x1u, nw_ref[1:2, :], NORM_EPS) * (1.0 + mod_ref[0, 4:5, :]) + mod_ref[0, 3:4, :]
        x1_parts.append(x1u)
        h_parts.append(hu.astype(BF16))
    x1 = jnp.concatenate(x1_parts, axis=0)
    h = jnp.concatenate(h_parts, axis=0)
    hidden = wg_ref.shape[1]
    f = None
    for c in range(hidden // FFN_CHUNK):
        cs = slice(c * FFN_CHUNK, (c + 1) * FFN_CHUNK)
        g = jnp.dot(h, wg_ref[:, cs], preferred_element_type=F32)
        u = jnp.dot(h, wu_ref[:, cs], preferred_element_type=F32)
        a = (_silu_exp2(g) * u).astype(BF16)
        fc = jnp.dot(a, wd_ref[cs, :], preferred_element_type=F32)
        f = fc if f is None else f + fc
    o_ref[0] = x1 + mod_ref[0, 5:6, :] * _rms(f, nw_ref[2:3, :], NORM_EPS)


def _outffn_call(x, o_hg, o_da, mod, norms, wo, wg, wu, wd):
    b, t, d = x.shape
    hidden = wg.shape[1]
    tm = min(TM_FFN, t)
    assert t % tm == 0 and hidden % FFN_CHUNK == 0

    def resident(shape):
        return pl.BlockSpec(shape, lambda i, j: (0, 0), pipeline_mode=pl.Buffered(1))

    return pl.pallas_call(
        _outffn_kernel,
        out_shape=jax.ShapeDtypeStruct((b, t, d), F32),
        grid=(b, t // tm),
        in_specs=[
            pl.BlockSpec((1, tm, d), lambda i, j: (i, j, 0)),
            pl.BlockSpec((1, tm, HG_WIDTH), lambda i, j: (i, j, 0)),
            pl.BlockSpec((1, tm, DA_WIDTH), lambda i, j: (i, j, 0)),
            pl.BlockSpec((1, N_MOD, d), lambda i, j: (i, 0, 0)),
            pl.BlockSpec((3, d), lambda i, j: (0, 0)),
            resident((HG_WIDTH + DA_WIDTH, d)),
            resident((d, hidden)),
            resident((d, hidden)),
            resident((hidden, d)),
        ],
        out_specs=pl.BlockSpec((1, tm, d), lambda i, j: (i, j, 0)),
        compiler_params=pltpu.CompilerParams(
            dimension_semantics=("parallel", "parallel"), vmem_limit_bytes=VMEM_LIMIT_BYTES),
        name="outproj_ffn",
    )(x, o_hg, o_da, mod, norms, wo, wg, wu, wd)


def _rotary_tables(t):
    dh = DA_HEAD_DIM
    inv = 1.0 / (ROPE_THETA ** (jnp.arange(0, dh, 2, dtype=F32) / dh))
    ang = jnp.arange(t, dtype=F32)[:, None] * inv[None, :]
    cos = jnp.cos(ang)
    sin = jnp.sin(ang)
    reps = LANES // dh
    cos_t = jnp.tile(jnp.concatenate([cos, cos], axis=-1), (1, reps))
    sin_t = jnp.tile(jnp.concatenate([-sin, sin], axis=-1), (1, reps))
    return cos_t, sin_t


def kernel(x_prompt, x_sample, c_prompt, c_sample, w_ada, b_ada, norm_pre_mix, norm_post_mix, norm_pre_ffn, norm_post_ffn, w_in, hg_lower_bounds, hg_gnorm, da_lambda_q1, da_lambda_k1, da_lambda_q2, da_lambda_k2, da_subln, w_out, w_ffn_gate, w_ffn_up, w_ffn_down):
    depth = w_in.shape[0]
    d = x_prompt.shape[-1]

    lb_soft = jax.nn.softmax(hg_lower_bounds.astype(F32), axis=1)
    lb_all = jnp.cumsum(lb_soft, axis=1) - lb_soft[:, :1]

    c_all = jnp.concatenate([c_prompt, c_sample], axis=0)
    mod_all = _ada_call(c_all, w_ada, b_ada)

    w_in_b = w_in.astype(BF16)
    w_out_b = w_out.astype(BF16)
    wg_b = w_ffn_gate.astype(BF16)
    wu_b = w_ffn_up.astype(BF16)
    wd_b = w_ffn_down.astype(BF16)

    def trunk(x, row0):
        b, t, _ = x.shape
        cos, sin = _rotary_tables(t)
        for l in range(depth):
            mod = mod_all[l, row0:row0 + b].reshape(b, N_MOD, d)
            lambda_init = 0.8 - 0.6 * math.exp(-0.3 * l)
            proj = _inproj_call(x, mod, norm_pre_mix[l], w_in_b[l], cos, sin)
            o_f = _hgrn_call(proj, lb_all[0, l], None, None, reverse=False)
            o_hg = _hgrn_call(proj, lb_all[1, l], hg_gnorm[l], o_f, reverse=True)
            o_da = _attn_call(proj, da_lambda_q1[l], da_lambda_k1[l], da_lambda_q2[l], da_lambda_k2[l],
                              da_subln[l], lambda_init)
            norms = jnp.stack([norm_post_mix[l], norm_pre_ffn[l], norm_post_ffn[l]], axis=0)
            x = _outffn_call(x, o_hg, o_da, mod, norms, w_out_b[l], wg_b[l], wu_b[l], wd_b[l])
        return x

    y_prompt = trunk(x_prompt, 0)
    y_sample = trunk(x_sample, c_prompt.shape[0])
    return (y_prompt, y_sample)
```

```python
import functools
import math

import numpy as np
import jax
import jax.numpy as jnp
from jax import lax
from jax.experimental import pallas as pl
from jax.experimental.pallas import tpu as pltpu

F32 = jnp.float32
BF16 = jnp.bfloat16

HG_WIDTH = 512
HG_HEADS = 4
HG_DK = 128
DA_WIDTH = 512
DA_HEADS = 4
DA_HEAD_DIM = 64
ROPE_THETA = 10000.0
NORM_EPS = 1e-6
SUBLN_EPS = 1e-5
LOG_FLOOR = 1e-30
LOG2_E = math.log2(math.e)
N_MOD = 6

LANES = 128
SUBLANES = 8
VMEM_LIMIT_BYTES = 56 * 1024 * 1024

COL_HQ, COL_HFF, COL_HFB, COL_HI, COL_HG, COL_DQ, COL_DK, COL_DV = range(8)
GROUP = 512

ADA_TN = 1536
TM_PROJ = 1024
TM_FFN = 1024
FFN_PRO_PARTS = 4
FFN_CHUNK = 256
HG_CHUNK = 128
HG_TILE = 512
HG_SUB = SUBLANES
HG_ROWS = 4
HG_UNROLL = 2
ATT_TQ = 512
ATT_TK = 1024
ATT_SUB = 2

NT_DIMS = (((1,), (1,)), ((), ()))
TN_DIMS = (((0,), (0,)), ((), ()))


def _silu(x):
    return x * jax.nn.sigmoid(x)


def _rms(x, w, eps):
    ms = jnp.mean(x * x, axis=-1, keepdims=True)
    return x * lax.rsqrt(ms + eps) * w


def _ada_kernel(c_ref, w_ref, b_ref, o_ref):
    s = _silu(c_ref[...]).astype(BF16)
    w = w_ref[0].astype(BF16)
    o_ref[0] = jnp.dot(s, w, preferred_element_type=F32) + b_ref[0]


def _ada_call(c_all, w_ada, b_ada):
    depth, d, n = w_ada.shape
    rows = c_all.shape[0]
    tn = ADA_TN
    assert n % tn == 0
    return pl.pallas_call(
        _ada_kernel,
        out_shape=jax.ShapeDtypeStruct((depth, rows, n), F32),
        grid=(depth, n // tn),
        in_specs=[
            pl.BlockSpec((rows, d), lambda l, j: (0, 0)),
            pl.BlockSpec((1, d, tn), lambda l, j: (l, 0, j)),
            pl.BlockSpec((1, 1, tn), lambda l, j: (l, 0, j)),
        ],
        out_specs=pl.BlockSpec((1, rows, tn), lambda l, j: (l, 0, j)),
        compiler_params=pltpu.CompilerParams(
            dimension_semantics=("parallel", "parallel"), vmem_limit_bytes=VMEM_LIMIT_BYTES),
        name="adaln_mod",
    )(c_all, w_ada, b_ada.reshape(depth, 1, n))


def _inproj_kernel(x_ref, mod_ref, nw_ref, w_ref, cos_ref, sin_ref, o_ref):
    x = x_ref[0]
    h = _rms(x, nw_ref[...], NORM_EPS) * (1.0 + mod_ref[0, 1:2, :]) + mod_ref[0, 0:1, :]
    h = h.astype(BF16)
    cos = cos_ref[...]
    sin = sin_ref[...]
    lane = lax.broadcasted_iota(jnp.int32, cos.shape, 1)
    first_half = (lane % DA_HEAD_DIM) < (DA_HEAD_DIM // 2)
    n_groups = w_ref.shape[1] // GROUP
    for n in range(n_groups):
        y = jnp.dot(h, w_ref[:, n * GROUP:(n + 1) * GROUP], preferred_element_type=F32)
        if n in (COL_DQ, COL_DK):
            pieces = []
            for j in range(GROUP // LANES):
                yj = y[:, j * LANES:(j + 1) * LANES]
                partner = jnp.where(first_half,
                                    pltpu.roll(yj, LANES - DA_HEAD_DIM // 2, 1),
                                    pltpu.roll(yj, DA_HEAD_DIM // 2, 1))
                r = yj * cos + partner * sin
                if n == COL_DQ:
                    r = r * (DA_HEAD_DIM ** -0.5 * LOG2_E)
                pieces.append(r)
            y = jnp.concatenate(pieces, axis=1)
        o_ref[0, :, n * GROUP:(n + 1) * GROUP] = y.astype(BF16)


def _inproj_call(x, mod, nw, w_bf16, cos, sin):
    b, t, d = x.shape
    n = w_bf16.shape[1]
    tm = min(TM_PROJ, t)
    assert t % tm == 0
    return pl.pallas_call(
        _inproj_kernel,
        out_shape=jax.ShapeDtypeStruct((b, t, n), BF16),
        grid=(b, t // tm),
        in_specs=[
            pl.BlockSpec((1, tm, d), lambda i, j: (i, j, 0)),
            pl.BlockSpec((1, N_MOD, d), lambda i, j: (i, 0, 0)),
            pl.BlockSpec((1, d), lambda i, j: (0, 0)),
            pl.BlockSpec((d, n), lambda i, j: (0, 0), pipeline_mode=pl.Buffered(1)),
            pl.BlockSpec((tm, LANES), lambda i, j: (j, 0)),
            pl.BlockSpec((tm, LANES), lambda i, j: (j, 0)),
        ],
        out_specs=pl.BlockSpec((1, tm, n), lambda i, j: (i, j, 0)),
        compiler_params=pltpu.CompilerParams(
            dimension_semantics=("parallel", "parallel"), vmem_limit_bytes=VMEM_LIMIT_BYTES),
        name="norm_inproj",
    )(x, mod, nw.reshape(1, d), w_bf16, cos, sin)


def _hgrn_levels(ch):
    levels = []
    m = HG_SUB
    while m < ch:
        levels.append(m)
        m *= 2
    return levels


@functools.lru_cache(maxsize=None)
def _hgrn_constants(ch, reverse):
    idx = np.arange(ch)
    nsb = ch // HG_SUB
    order = (idx % nsb) * HG_SUB + idx // nsb
    if reverse:
        cum = (idx[None, :] >= idx[:, None]).astype(np.float32)
    else:
        cum = (idx[None, :] <= idx[:, None]).astype(np.float32)
    refs = []
    masks = []
    for m in _hgrn_levels(ch):
        pair = idx // (2 * m)
        in_second = (idx % (2 * m)) >= m
        ref = pair * 2 * m + (m if reverse else m - 1)
        refs.append(cum[ref[::HG_SUB]])
        later = ~in_second if reverse else in_second
        earlier = ~later
        masks.append(((pair[:, None] == pair[None, :]) & later[:, None] & earlier[None, :]).astype(np.float32))
    sel = np.ix_(order, order)
    w = np.concatenate([cum[sel]] + [rf[:, order] for rf in refs], axis=0)
    w = np.concatenate([w, w], axis=1)
    masks = np.stack([mk[sel] for mk in masks], axis=0)
    perm = np.eye(ch, dtype=np.float32)[order]
    return perm, w, masks


def _silu_exp2(x):
    return x / (1.0 + jnp.exp2(-LOG2_E * x))


def _hgrn_kernel(*refs, reverse, final, ch, n_chunks):
    if final:
        (hq_ref, hf_ref, hi_ref, hg_ref, of_ref, lb_ref, gw_ref, perm_ref, permt_ref, w_ref, mask_ref,
         o_ref, st_ref) = refs
    else:
        (hq_ref, hf_ref, hi_ref, lb_ref, perm_ref, w_ref, mask_ref, o_ref, st_ref) = refs
    n_lev = len(_hgrn_levels(ch))
    nsb = ch // HG_SUB

    @pl.when(pl.program_id(1) == 0)
    def _():
        st_ref[...] = jnp.zeros_like(st_ref)

    lb = lb_ref[...]
    lb_floor = jnp.maximum(lb, LOG_FLOOR)
    one_m_lb = 1.0 - lb
    ones = jnp.ones((LANES, LANES), BF16)

    def slab(x, t):
        return x[t * nsb:(t + 1) * nsb]

    nb = hq_ref.shape[0]
    n_groups = 4 if final else 3
    units = [(bi, h) for bi in range(nb) for h in range(HG_HEADS)]
    pairs = [(t, s) for t in range(HG_SUB) for s in range(HG_SUB) if ((s >= t) if reverse else (s <= t))]

    def chunk_body(i, carry):
        c = (n_chunks - 1 - i) if reverse else i
        rows = pl.ds(pl.multiple_of(c * ch, ch), ch)
        parts = []
        for bi in range(nb):
            parts += [hq_ref[bi, rows, :], hf_ref[bi, rows, :], hi_ref[bi, rows, :]]
            if final:
                parts.append(hg_ref[bi, rows, :])
        xp = jnp.dot(perm_ref[...], jnp.concatenate(parts, axis=1), preferred_element_type=F32)

        def col(group, bi, h):
            c0 = (bi * n_groups + group) * HG_WIDTH + h * HG_DK
            return xp[:, c0:c0 + HG_DK]

        q, kk, qb, kb, v, vb, g_hi, g_lo = [], [], [], [], [], [], [], []
        for bi, h in units:
            cols = slice(h * HG_DK, (h + 1) * HG_DK)
            z = col(1, bi, h)
            e = jnp.exp2(-LOG2_E * jnp.abs(z))
            r = 1.0 / (1.0 + e)
            er = e * r
            pos = z >= 0.0
            g = jnp.log2(lb_floor[:, cols] + one_m_lb[:, cols] * jnp.where(pos, r, er))
            kk.append(one_m_lb[:, cols] * jnp.where(pos, er, r))
            q.append(_silu_exp2(col(0, bi, h)))
            qb.append(q[-1].astype(BF16))
            kb.append(kk[-1].astype(BF16))
            v.append(col(2, bi, h))
            vb.append(col(2, bi, h).astype(BF16))
            g_hi.append(g.astype(BF16))
            g_lo.append((g - g_hi[-1].astype(F32)).astype(BF16))

        g2 = jnp.concatenate([jnp.concatenate(g_hi, axis=1), jnp.concatenate(g_lo, axis=1)], axis=0)
        y_all = jnp.dot(w_ref[...], g2, preferred_element_type=F32)
        y = [y_all[:, u * HG_DK:(u + 1) * HG_DK] for u in range(len(units))]
        b = [yu[0:ch] for yu in y]
        b_end = [bu[0:1] if reverse else bu[ch - 1:ch] for bu in b]

        a = []
        for u in range(len(units)):
            au = None
            for l in range(n_lev):
                ref = y[u][ch + l * nsb:ch + (l + 1) * nsb]
                el = jnp.exp2(-jnp.abs(b[u] - jnp.concatenate([ref] * HG_SUB, axis=0))).astype(BF16)
                al = lax.dot_general(qb[u] * el, kb[u] * el, NT_DIMS,
                                     preferred_element_type=F32) * mask_ref[l]
                au = al if au is None else au + al
            a.append(au.astype(BF16))

        pieces = []
        for u in range(len(units)):
            for t, s in pairs:
                p = slab(qb[u], t) * slab(kb[u], s)
                if s != t:
                    p = p * jnp.exp2(slab(b[u], t) - slab(b[u], s)).astype(BF16)
                pieces.append(p)
        rsum = jnp.dot(jnp.concatenate(pieces, axis=0), ones, preferred_element_type=F32)

        outs = []
        for u, (bi, h) in enumerate(units):
            cols = slice(h * HG_DK, (h + 1) * HG_DK)
            od = [None] * HG_SUB
            for n, (t, s) in enumerate(pairs):
                n0 = (u * len(pairs) + n) * nsb
                term = rsum[n0:n0 + nsb] * slab(v[u], s)
                od[t] = term if od[t] is None else od[t] + term
            qs = (q[u] * jnp.exp2(b[u])).astype(BF16)
            ks = (kk[u] * jnp.exp2(b_end[u] - b[u])).astype(BF16)
            st = st_ref[bi, h]
            o = (jnp.dot(a[u], vb[u], preferred_element_type=F32) + jnp.concatenate(od, axis=0)
                 + lax.dot_general(qs, st.astype(BF16), NT_DIMS, preferred_element_type=F32))
            st_ref[bi, h] = (st * jnp.exp2(b_end[u])
                             + lax.dot_general(vb[u], ks, TN_DIMS, preferred_element_type=F32))
            if final:
                o = o + of_ref[bi, rows, cols]
                o = _rms(o, gw_ref[...], NORM_EPS) * _silu_exp2(col(3, bi, h))
                outs.append(o.astype(BF16))
            else:
                o_ref[bi, rows, cols] = o
        if final:
            y_out = jnp.dot(permt_ref[...], jnp.concatenate(outs, axis=1), preferred_element_type=F32)
            for bi in range(nb):
                o_ref[bi, rows, :] = y_out[:, bi * HG_WIDTH:(bi + 1) * HG_WIDTH].astype(o_ref.dtype)
        return carry

    lax.fori_loop(0, n_chunks, chunk_body, 0, unroll=HG_UNROLL)


def _hgrn_call(proj, lb_dir, gnorm_w, o_fwd, *, reverse):
    b, t, _ = proj.shape
    final = o_fwd is not None
    ch = HG_CHUNK
    tile = min(HG_TILE, t)
    assert t % tile == 0 and tile % ch == 0
    nt = t // tile
    perm_np, w_np, mask_np = _hgrn_constants(ch, reverse)

    def tok(j):
        return (nt - 1 - j) if reverse else j

    nb = HG_ROWS if b % HG_ROWS == 0 else 1

    def col_spec(group):
        return pl.BlockSpec((nb, tile, GROUP), lambda i, j: (i, tok(j), group))

    def const_spec(shape):
        return pl.BlockSpec(shape, lambda i, j: (0,) * len(shape))

    in_specs = [col_spec(COL_HQ), col_spec(COL_HFB if reverse else COL_HFF), col_spec(COL_HI)]
    args = [proj, proj, proj]
    if final:
        in_specs += [col_spec(COL_HG), pl.BlockSpec((nb, tile, HG_WIDTH), lambda i, j: (i, tok(j), 0))]
        args += [proj, o_fwd]
    in_specs.append(const_spec((1, HG_WIDTH)))
    args.append(lb_dir.reshape(1, HG_WIDTH))
    if final:
        in_specs.append(const_spec((1, HG_DK)))
        args.append(gnorm_w.reshape(1, HG_DK))
    in_specs.append(const_spec(perm_np.shape))
    args.append(jnp.asarray(perm_np, BF16))
    if final:
        in_specs.append(const_spec(perm_np.shape))
        args.append(jnp.asarray(perm_np.T, BF16))
    in_specs += [const_spec(w_np.shape), const_spec(mask_np.shape)]
    args += [jnp.asarray(w_np, BF16), jnp.asarray(mask_np, F32)]

    kern = functools.partial(_hgrn_kernel, reverse=reverse, final=final, ch=ch, n_chunks=tile // ch)
    return pl.pallas_call(
        kern,
        out_shape=jax.ShapeDtypeStruct((b, t, HG_WIDTH), BF16 if final else F32),
        grid=(b // nb, nt),
        in_specs=in_specs,
        out_specs=pl.BlockSpec((nb, tile, HG_WIDTH), lambda i, j: (i, tok(j), 0)),
        scratch_shapes=[pltpu.VMEM((nb, HG_HEADS, HG_DK, HG_DK), F32)],
        compiler_params=pltpu.CompilerParams(
            dimension_semantics=("parallel", "arbitrary"), vmem_limit_bytes=VMEM_LIMIT_BYTES),
        name="hgrn2_bwd" if reverse else "hgrn2_fwd",
    )(*args)


def _attn_kernel(q_ref, k_ref, v_ref, lq1_ref, lk1_ref, lq2_ref, lk2_ref, sw_ref, o_ref,
                 *, tq, tkc, n_kv, n_sub, lambda_init):
    for u in range(n_sub):
        _attn_tile(q_ref, k_ref, v_ref, lq1_ref, lk1_ref, lq2_ref, lk2_ref, sw_ref, o_ref,
                   slice(u * tq, (u + 1) * tq), tq=tq, tkc=tkc, n_kv=n_kv, lambda_init=lambda_init)


def _attn_tile(q_ref, k_ref, v_ref, lq1_ref, lk1_ref, lq2_ref, lk2_ref, sw_ref, o_ref, q_rows,
               *, tq, tkc, n_kv, lambda_init):
    q = q_ref[0, q_rows, :]
    lane = lax.broadcasted_iota(jnp.int32, q.shape, 1)
    zero = jnp.zeros_like(q)
    qs = jnp.concatenate([jnp.where(lane < DA_HEAD_DIM, q, zero),
                          jnp.where(lane >= DA_HEAD_DIM, q, zero)], axis=0)

    def body(j, carry):
        m_prev, l_prev, acc = carry
        rows = pl.ds(pl.multiple_of(j * tkc, tkc), tkc)
        k = k_ref[0, rows, :]
        v = v_ref[0, rows, :]
        s = lax.dot_general(k, qs, NT_DIMS, preferred_element_type=F32)
        m_new = jnp.maximum(m_prev, jnp.max(s, axis=0, keepdims=True))
        alpha = jnp.exp2(m_prev - m_new)
        p = jnp.exp2(s - m_new)
        l_new = alpha * l_prev + jnp.sum(p, axis=0, keepdims=True)
        acc = alpha * acc + lax.dot_general(v, p.astype(BF16), TN_DIMS, preferred_element_type=F32)
        return m_new, l_new, acc

    init = (jnp.full((1, 2 * tq), -jnp.inf, F32), jnp.zeros((1, 2 * tq), F32),
            jnp.zeros((2 * DA_HEAD_DIM, 2 * tq), F32))
    _, l_fin, acc = lax.fori_loop(0, n_kv, body, init, unroll=True)

    o = acc / l_fin
    lam = (jnp.exp(jnp.sum(lq1_ref[...] * lk1_ref[...], axis=-1, keepdims=True))
           - jnp.exp(jnp.sum(lq2_ref[...] * lk2_ref[...], axis=-1, keepdims=True)) + lambda_init)
    od = (o[:, 0:tq] - lam * o[:, tq:2 * tq]).T
    od = _rms(od, sw_ref[...], SUBLN_EPS) * (1.0 - lambda_init)
    o_ref[0, q_rows, :] = od.astype(o_ref.dtype)


def _attn_call(proj, lq1, lk1, lq2, lk2, subln_w, lambda_init):
    b, t, _ = proj.shape
    tq = min(ATT_TQ, t)
    tkc = min(ATT_TK, t)
    n_sub = ATT_SUB if t % (ATT_SUB * tq) == 0 else 1
    assert t % (n_sub * tq) == 0 and t % tkc == 0
    head_w = 2 * DA_HEAD_DIM
    blocks_per_group = GROUP // head_w

    def lam_spec():
        return pl.BlockSpec((1, DA_HEAD_DIM), lambda i, h, qi: (0, 0))

    kern = functools.partial(_attn_kernel, tq=tq, tkc=tkc, n_kv=t // tkc, n_sub=n_sub, lambda_init=lambda_init)
    return pl.pallas_call(
        kern,
        out_shape=jax.ShapeDtypeStruct((b, t, DA_WIDTH), BF16),
        grid=(b, DA_HEADS, t // (n_sub * tq)),
        in_specs=[
            pl.BlockSpec((1, n_sub * tq, head_w), lambda i, h, qi: (i, qi, COL_DQ * blocks_per_group + h)),
            pl.BlockSpec((1, t, head_w), lambda i, h, qi: (i, 0, COL_DK * blocks_per_group + h)),
            pl.BlockSpec((1, t, head_w), lambda i, h, qi: (i, 0, COL_DV * blocks_per_group + h)),
            lam_spec(), lam_spec(), lam_spec(), lam_spec(),
            pl.BlockSpec((1, head_w), lambda i, h, qi: (0, 0)),
        ],
        out_specs=pl.BlockSpec((1, n_sub * tq, head_w), lambda i, h, qi: (i, qi, h)),
        compiler_params=pltpu.CompilerParams(
            dimension_semantics=("parallel", "parallel", "arbitrary"),
            vmem_limit_bytes=VMEM_LIMIT_BYTES),
        name="diff_attn",
    )(proj, proj, proj, lq1.reshape(1, -1), lk1.reshape(1, -1), lq2.reshape(1, -1), lk2.reshape(1, -1),
      subln_w.reshape(1, -1))


def _outffn_kernel(x_ref, ohg_ref, oda_ref, mod_ref, nw_ref, wo_ref, wg_ref, wu_ref, wd_ref, o_ref):
    rows_part = x_ref.shape[1] // FFN_PRO_PARTS
    x1_parts, h_parts = [], []
    for u in range(FFN_PRO_PARTS):
        rows = slice(u * rows_part, (u + 1) * rows_part)
        mix = (jnp.dot(ohg_ref[0, rows, :], wo_ref[0:HG_WIDTH, :], preferred_element_type=F32)
               + jnp.dot(oda_ref[0, rows, :], wo_ref[HG_WIDTH:HG_WIDTH + DA_WIDTH, :],
                         preferred_element_type=F32))
        x1u = x_ref[0, rows, :] + mod_ref[0, 2:3, :] * _rms(mix, nw_ref[0:1, :], NORM_EPS)
        hu = _rms(x1u, nw_ref[1:2, :], NORM_EPS) * (1.0 + mod_ref[0, 4:5, :]) + mod_ref[0, 3:4, :]
        x1_parts.append(x1u)
        h_parts.append(hu.astype(BF16))
    x1 = jnp.concatenate(x1_parts, axis=0)
    h = jnp.concatenate(h_parts, axis=0)
    hidden = wg_ref.shape[1]
    f = None
    for c in range(hidden // FFN_CHUNK):
        cs = slice(c * FFN_CHUNK, (c + 1) * FFN_CHUNK)
        g = jnp.dot(h, wg_ref[:, cs], preferred_element_type=F32)
        u = jnp.dot(h, wu_ref[:, cs], preferred_element_type=F32)
        a = (_silu_exp2(g) * u).astype(BF16)
        fc = jnp.dot(a, wd_ref[cs, :], preferred_element_type=F32)
        f = fc if f is None else f + fc
    o_ref[0] = x1 + mod_ref[0, 5:6, :] * _rms(f, nw_ref[2:3, :], NORM_EPS)


def _outffn_call(x, o_hg, o_da, mod, norms, wo, wg, wu, wd):
    b, t, d = x.shape
    hidden = wg.shape[1]
    tm = min(TM_FFN, t)
    assert t % tm == 0 and hidden % FFN_CHUNK == 0

    def resident(shape):
        return pl.BlockSpec(shape, lambda i, j: (0, 0), pipeline_mode=pl.Buffered(1))

    return pl.pallas_call(
        _outffn_kernel,
        out_shape=jax.ShapeDtypeStruct((b, t, d), F32),
        grid=(b, t // tm),
        in_specs=[
            pl.BlockSpec((1, tm, d), lambda i, j: (i, j, 0)),
            pl.BlockSpec((1, tm, HG_WIDTH), lambda i, j: (i, j, 0)),
            pl.BlockSpec((1, tm, DA_WIDTH), lambda i, j: (i, j, 0)),
            pl.BlockSpec((1, N_MOD, d), lambda i, j: (i, 0, 0)),
            pl.BlockSpec((3, d), lambda i, j: (0, 0)),
            resident((HG_WIDTH + DA_WIDTH, d)),
            resident((d, hidden)),
            resident((d, hidden)),
            resident((hidden, d)),
        ],
        out_specs=pl.BlockSpec((1, tm, d), lambda i, j: (i, j, 0)),
        compiler_params=pltpu.CompilerParams(
            dimension_semantics=("parallel", "parallel"), vmem_limit_bytes=VMEM_LIMIT_BYTES),
        name="outproj_ffn",
    )(x, o_hg, o_da, mod, norms, wo, wg, wu, wd)


def _rotary_tables(t):
    dh = DA_HEAD_DIM
    inv = 1.0 / (ROPE_THETA ** (jnp.arange(0, dh, 2, dtype=F32) / dh))
    ang = jnp.arange(t, dtype=F32)[:, None] * inv[None, :]
    cos = jnp.cos(ang)
    sin = jnp.sin(ang)
    reps = LANES // dh
    cos_t = jnp.tile(jnp.concatenate([cos, cos], axis=-1), (1, reps))
    sin_t = jnp.tile(jnp.concatenate([-sin, sin], axis=-1), (1, reps))
    return cos_t, sin_t


def kernel(x_prompt, x_sample, c_prompt, c_sample, w_ada, b_ada, norm_pre_mix, norm_post_mix, norm_pre_ffn, norm_post_ffn, w_in, hg_lower_bounds, hg_gnorm, da_lambda_q1, da_lambda_k1, da_lambda_q2, da_lambda_k2, da_subln, w_out, w_ffn_gate, w_ffn_up, w_ffn_down):
    depth = w_in.shape[0]
    d = x_prompt.shape[-1]

    lb_soft = jax.nn.softmax(hg_lower_bounds.astype(F32), axis=1)
    lb_all = jnp.cumsum(lb_soft, axis=1) - lb_soft[:, :1]

    c_all = jnp.concatenate([c_prompt, c_sample], axis=0)
    mod_all = _ada_call(c_all, w_ada, b_ada)

    w_in_b = w_in.astype(BF16)
    w_out_b = w_out.astype(BF16)
    wg_b = w_ffn_gate.astype(BF16)
    wu_b = w_ffn_up.astype(BF16)
    wd_b = w_ffn_down.astype(BF16)

    def trunk(x, row0):
        b, t, _ = x.shape
        cos, sin = _rotary_tables(t)
        for l in range(depth):
            mod = mod_all[l, row0:row0 + b].reshape(b, N_MOD, d)
            lambda_init = 0.8 - 0.6 * math.exp(-0.3 * l)
            proj = _inproj_call(x, mod, norm_pre_mix[l], w_in_b[l], cos, sin)
            o_f = _hgrn_call(proj, lb_all[0, l], None, None, reverse=False)
            o_hg = _hgrn_call(proj, lb_all[1, l], hg_gnorm[l], o_f, reverse=True)
            o_da = _attn_call(proj, da_lambda_q1[l], da_lambda_k1[l], da_lambda_q2[l], da_lambda_k2[l],
                              da_subln[l], lambda_init)
            norms = jnp.stack([norm_post_mix[l], norm_pre_ffn[l], norm_post_ffn[l]], axis=0)
            x = _outffn_call(x, o_hg, o_da, mod, norms, w_out_b[l], wg_b[l], wu_b[l], wd_b[l])
        return x

    y_prompt = trunk(x_prompt, 0)
    y_sample = trunk(x_sample, c_prompt.shape[0])
    return (y_prompt, y_sample)
```
